```python
import math
import jax, jax.numpy as jnp
from jax import lax
import numpy as np

D_MODEL = 1024
BATCH = 32
SEQ = 256
DEPTH = 4
DEC_BATCH = 4
DEC_SEQ = 4096
PAST_LEN = 512

GRID_W = 64
HEAD_DIM = 64
ROPE_PAIRS = HEAD_DIM // 4
ROPE_THETA = 10000.0
Q_BLOCK = 128
EPS = 1e-6
DA_HEADS = 4
DA_VDIM = 2 * HEAD_DIM
DA_QK_W = DA_HEADS * 2 * HEAD_DIM
DA_W = DA_HEADS * DA_VDIM
ML_HEADS = 4
ML_DIM = 128
ML_W = ML_HEADS * ML_DIM
ML_CHUNK = 128
N_GATES = 2 * 2 * ML_HEADS
EVEN_WIDTHS = (DA_QK_W, DA_QK_W, DA_W, ML_W, ML_W, ML_W, ML_W, N_GATES)
EVEN_IN = 2 * DA_QK_W + DA_W + 4 * ML_W + N_GATES
MIX_W = DA_W + ML_W
GQA_Q_HEADS = 16
GQA_KV_HEADS = 4
GQA_GROUP = GQA_Q_HEADS // GQA_KV_HEADS
ODD_WIDTHS = (GQA_Q_HEADS * HEAD_DIM, GQA_KV_HEADS * HEAD_DIM, GQA_KV_HEADS * HEAD_DIM)
ODD_IN = (GQA_Q_HEADS + 2 * GQA_KV_HEADS) * HEAD_DIM
D_FF = -(-8 * D_MODEL // (3 * 256)) * 256
N_EVEN = (DEPTH + 1) // 2
N_ODD = DEPTH // 2
F32 = jnp.float32

kernel_name = 'hybrid_flow_diffattn_mlstm_gqa_step'


def rms_norm(x, g):
    xf = x.astype(F32)
    y = xf * lax.rsqrt(jnp.mean(xf * xf, axis=-1, keepdims=True) + EPS)
    return (y * g.astype(F32)).astype(x.dtype)


def split_cols(p, widths):
    outs, start = [], 0
    for w in widths:
        outs.append(p[..., start:start + w])
        start += w
    return outs


def axial_rope_tables(n_tokens):
    rows = n_tokens // GRID_W
    row = jnp.repeat(jnp.arange(rows, dtype=F32), GRID_W)
    col = jnp.tile(jnp.arange(GRID_W, dtype=F32), rows)
    freqs = ROPE_THETA ** (-jnp.arange(ROPE_PAIRS, dtype=F32) / ROPE_PAIRS)
    ang = jnp.stack([row[:, None] * freqs, col[:, None] * freqs], axis=1)
    return jnp.cos(ang), jnp.sin(ang)


def apply_rope(x, cos, sin):
    extra = x.ndim - 3
    shp = (x.shape[1],) + (1,) * extra + (2, ROPE_PAIRS)
    cs = cos.reshape(shp).astype(x.dtype)
    sn = sin.reshape(shp).astype(x.dtype)
    xr = x.reshape(x.shape[:-1] + (2, 2, ROPE_PAIRS))
    x0, x1 = xr[..., 0, :], xr[..., 1, :]
    out = jnp.stack([x0 * cs - x1 * sn, x0 * sn + x1 * cs], axis=-2)
    return out.reshape(x.shape)


def block_attention(q, k, v, mix):
    b, sq, hk, g, nc, d = q.shape
    nb = sq // Q_BLOCK
    qb = jnp.moveaxis(q.reshape(b, nb, Q_BLOCK, hk, g, nc, d), 1, 0)
    kf = k.astype(F32) * (d ** -0.5)
    vf = v.astype(F32)

    def one_block(qi):
        s = jnp.einsum('bqhgcd,bkhcd->bhgcqk', qi.astype(F32), kf)
        p = jnp.einsum('bhgcqk,c->bhgqk', jax.nn.softmax(s, axis=-1), mix)
        return jnp.einsum('bhgqk,bkhe->bqhge', p, vf).astype(v.dtype)

    out = lax.map(one_block, qb)
    return jnp.moveaxis(out, 0, 1).reshape(b, sq, hk, g, v.shape[-1])


def mlstm_scan(q, k, v, ig, fg, C0, n0, m0):
    b, s, h, d = q.shape
    nc, L = s // ML_CHUNK, ML_CHUNK

    def to_chunks(a):
        return jnp.moveaxis(a.reshape((b, nc, L) + a.shape[2:]), 1, 0)

    qc = to_chunks(q.astype(F32))
    kc = to_chunks(k.astype(F32) * (d ** -0.5))
    vc = to_chunks(v.astype(F32))
    ic = to_chunks(ig.astype(F32))
    lfc = to_chunks(jax.nn.log_sigmoid(fg.astype(F32)))
    causal = jnp.tril(jnp.ones((L, L), dtype=bool))[None, :, :, None]

    def step(carry, xs):
        C, n, m = carry
        qt, kt, vt, it, lf = xs
        bcum = jnp.cumsum(lf, axis=1)
        dmat = bcum[:, :, None, :] - bcum[:, None, :, :] + it[:, None, :, :]
        dmat = jnp.where(causal, dmat, -jnp.inf)
        inter = bcum + m[:, None, :]
        m_t = jnp.maximum(inter, jnp.max(dmat, axis=2))
        w_intra = jnp.exp(dmat - m_t[:, :, None, :])
        w_inter = jnp.exp(inter - m_t)
        qk = jnp.einsum('bthd,bshd->btsh', qt, kt) * w_intra
        num = jnp.einsum('btsh,bshe->bthe', qk, vt) + w_inter[..., None] * jnp.einsum('bhed,bthd->bthe', C, qt)
        den = jnp.sum(qk, axis=2) + w_inter * jnp.einsum('bhd,bthd->bth', n, qt)
        hout = num / jnp.maximum(jnp.abs(den), jnp.exp(-m_t))[..., None]
        m_new = m_t[:, -1]
        w_s = jnp.exp(bcum[:, -1:, :] - bcum + it - m_new[:, None, :])
        decay = jnp.exp(bcum[:, -1] + m - m_new)
        C_new = decay[..., None, None] * C + jnp.einsum('bsh,bshe,bshd->bhed', w_s, vt, kt)
        n_new = decay[..., None] * n + jnp.einsum('bsh,bshd->bhd', w_s, kt)
        return (C_new, n_new, m_new), hout

    init = (C0.astype(F32), n0.astype(F32), m0.astype(F32))
    (C, n, m), hs = lax.scan(step, init, (qc, kc, vc, ic, lfc))
    hs = jnp.moveaxis(hs, 0, 1).reshape(b, s, h, d)
    return hs.astype(q.dtype), C.astype(q.dtype), n.astype(q.dtype), m.astype(q.dtype)


def mlstm_bidir(q, k, v, gates, C0, n0, m0):
    h_f, Cf, nf, mf = mlstm_scan(q, k, v, gates[:, :, 0, 0], gates[:, :, 0, 1], C0[:, 0], n0[:, 0], m0[:, 0])
    rev = lambda a: jnp.flip(a, axis=1)
    h_b, Cb, nb, mb = mlstm_scan(rev(q), rev(k), rev(v), rev(gates[:, :, 1, 0]), rev(gates[:, :, 1, 1]),
                                 C0[:, 1], n0[:, 1], m0[:, 1])
    h = h_f + rev(h_b)
    return h, jnp.stack([Cf, Cb], axis=1), jnp.stack([nf, nb], axis=1), jnp.stack([mf, mb], axis=1)


def even_mixer(h, li, P, rope, ctx):
    lj = li // 2
    b, s, _ = h.shape
    lam_init = 0.8 - 0.6 * math.exp(-0.3 * li)
    p = h @ P['w_in_even'][lj]
    da_q, da_k, da_v, ml_q, ml_k, ml_v, ml_o, gts = split_cols(p, EVEN_WIDTHS)
    da_q = rms_norm(da_q.reshape(b, s, DA_HEADS, 2, HEAD_DIM), P['qk_gain_even'][lj, 0])
    da_k = rms_norm(da_k.reshape(b, s, DA_HEADS, 2, HEAD_DIM), P['qk_gain_even'][lj, 1])
    da_v = da_v.reshape(b, s, DA_HEADS, DA_VDIM)
    ml_q, ml_k, ml_v = [a.reshape(b, s, ML_HEADS, ML_DIM) for a in (ml_q, ml_k, ml_v)]
    gts = gts.reshape(b, s, 2, 2, ML_HEADS) + P['b_gate_even'][lj]
    if ctx is None:
        k_all, v_all = da_k, da_v
        C0 = jnp.zeros((b, 2, ML_HEADS, ML_DIM, ML_DIM), F32)
        n0 = jnp.zeros((b, 2, ML_HEADS, ML_DIM), F32)
        m0 = jnp.zeros((b, 2, ML_HEADS), F32)
    else:
        ck, cv, C0, n0, m0 = ctx
        cos, sin = rope
        da_q = apply_rope(da_q, cos, sin)
        k_all = jnp.concatenate([apply_rope(da_k, cos, sin), ck.astype(da_k.dtype)], axis=1)
        v_all = jnp.concatenate([da_v, cv.astype(da_v.dtype)], axis=1)
    lam = P['lam_even'][lj].astype(F32)
    lam_full = jnp.exp(jnp.sum(lam[0] * lam[1])) - jnp.exp(jnp.sum(lam[2] * lam[3])) + lam_init
    mix = jnp.stack([jnp.ones((), F32), -lam_full])
    da_o = block_attention(da_q[:, :, :, None], k_all, v_all, mix).reshape(b, s, DA_HEADS, DA_VDIM)
    da_o = rms_norm(da_o, P['da_norm_even'][lj]) * (1.0 - lam_init)
    ml_h, Cn, nn_, mn = mlstm_bidir(ml_q, ml_k, ml_v, gts, C0, n0, m0)
    ml_out = rms_norm(ml_h, P['ml_norm_even'][lj]).reshape(b, s, ML_W) * jax.nn.sigmoid(ml_o)
    out = jnp.concatenate([da_o.reshape(b, s, DA_W), ml_out], axis=-1) @ P['w_out_even'][lj]
    new_ctx = (da_k, da_v, Cn, nn_, mn) if ctx is None else None
    return out, new_ctx


def odd_mixer(h, li, P, rope, ctx):
    lj = li // 2
    b, s, _ = h.shape
    q, k, v = split_cols(h @ P['w_in_odd'][lj], ODD_WIDTHS)
    q = rms_norm(q.reshape(b, s, GQA_Q_HEADS, HEAD_DIM), P['qk_gain_odd'][lj, 0])
    k = rms_norm(k.reshape(b, s, GQA_KV_HEADS, HEAD_DIM), P['qk_gain_odd'][lj, 1])
    v = v.reshape(b, s, GQA_KV_HEADS, HEAD_DIM)
    if ctx is None:
        k_all, v_all = k, v
        new_ctx = (k, v)
    else:
        ck, cv = ctx
        cos, sin = rope
        q = apply_rope(q, cos, sin)
        k_all = jnp.concatenate([apply_rope(k, cos, sin), ck.astype(k.dtype)], axis=1)
        v_all = jnp.concatenate([v, cv.astype(v.dtype)], axis=1)
        new_ctx = None
    qg = q.reshape(b, s, GQA_KV_HEADS, GQA_GROUP, 1, HEAD_DIM)
    o = block_attention(qg, k_all[:, :, :, None], v_all, jnp.ones((1,), F32))
    return o.reshape(b, s, GQA_Q_HEADS * HEAD_DIM) @ P['w_out_odd'][lj], new_ctx


def swiglu(h, w_in, w_out):
    gate, up = jnp.split(h @ w_in, 2, axis=-1)
    return (jax.nn.silu(gate) * up) @ w_out


def trunk_layer(x, cond, li, P, rope, ctx):
    mod = jax.nn.silu(cond) @ P['w_mod'][li] + P['b_mod'][li]
    sh1, sc1, g1, sh2, sc2, g2 = jnp.split(mod[:, None, :], 6, axis=-1)
    h = rms_norm(x, P['norm1'][li]) * (1 + sc1) + sh1
    if li % 2 == 0:
        mo, new_ctx = even_mixer(h, li, P, rope, ctx)
    else:
        mo, new_ctx = odd_mixer(h, li, P, rope, ctx)
    x = x + g1 * mo
    h = rms_norm(x, P['norm2'][li]) * (1 + sc2) + sh2
    x = x + g2 * swiglu(h, P['w_ffn_in'][li], P['w_ffn_out'][li])
    return x, new_ctx


def setup_inputs(seed: int = 0) -> dict:
    key = jax.random.key(seed)
    ks = jax.random.split(key, 32)
    nrm = lambda k, shape, s=1.0: s * jax.random.normal(k, shape, F32)
    b_gate = nrm(ks[16], (N_EVEN, 2, 2, ML_HEADS), 0.1)
    b_gate = b_gate.at[:, :, 1, :].add(jnp.linspace(3.0, 6.0, ML_HEADS, dtype=F32))
    return {
        'x_prompt': nrm(ks[0], (BATCH, SEQ, D_MODEL)),
        'x_sample': nrm(ks[1], (DEC_BATCH, DEC_SEQ, D_MODEL)),
        'c': nrm(ks[2], (DEC_BATCH, D_MODEL)),
        'cache_da_k': nrm(ks[3], (DEC_BATCH, N_EVEN, PAST_LEN, DA_HEADS, 2, HEAD_DIM)),
        'cache_da_v': nrm(ks[4], (DEC_BATCH, N_EVEN, PAST_LEN, DA_HEADS, DA_VDIM)),
        'state_mlstm_C': nrm(ks[5], (DEC_BATCH, N_EVEN, 2, ML_HEADS, ML_DIM, ML_DIM), 0.05),
        'state_mlstm_n': nrm(ks[6], (DEC_BATCH, N_EVEN, 2, ML_HEADS, ML_DIM), 0.1),
        'state_mlstm_m': nrm(ks[7], (DEC_BATCH, N_EVEN, 2, ML_HEADS)),
        'cache_gqa_k': nrm(ks[8], (DEC_BATCH, N_ODD, PAST_LEN, GQA_KV_HEADS, HEAD_DIM)),
        'cache_gqa_v': nrm(ks[9], (DEC_BATCH, N_ODD, PAST_LEN, GQA_KV_HEADS, HEAD_DIM)),
        'c_ctx': nrm(ks[10], (D_MODEL,)),
        'norm1': 1.0 + nrm(ks[11], (DEPTH, D_MODEL), 0.02),
        'norm2': 1.0 + nrm(ks[12], (DEPTH, D_MODEL), 0.02),
        'w_mod': nrm(ks[13], (DEPTH, D_MODEL, 6 * D_MODEL), 0.5 * D_MODEL ** -0.5),
        'b_mod': nrm(ks[14], (DEPTH, 6 * D_MODEL), 0.02),
        'w_in_even': nrm(ks[15], (N_EVEN, D_MODEL, EVEN_IN), D_MODEL ** -0.5),
        'b_gate_even': b_gate,
        'qk_gain_even': 1.0 + nrm(ks[17], (N_EVEN, 2, 2, HEAD_DIM), 0.02),
        'lam_even': nrm(ks[18], (N_EVEN, 4, HEAD_DIM), 0.1),
        'da_norm_even': 1.0 + nrm(ks[19], (N_EVEN, DA_VDIM), 0.02),
        'ml_norm_even': 1.0 + nrm(ks[20], (N_EVEN, ML_DIM), 0.02),
        'w_out_even': nrm(ks[21], (N_EVEN, MIX_W, D_MODEL), MIX_W ** -0.5),
        'w_in_odd': nrm(ks[22], (N_ODD, D_MODEL, ODD_IN), D_MODEL ** -0.5),
        'qk_gain_odd': 1.0 + nrm(ks[23], (N_ODD, 2, HEAD_DIM), 0.02),
        'w_out_odd': nrm(ks[24], (N_ODD, GQA_Q_HEADS * HEAD_DIM, D_MODEL), (GQA_Q_HEADS * HEAD_DIM) ** -0.5),
        'w_ffn_in': nrm(ks[25], (DEPTH, D_MODEL, 2 * D_FF), D_MODEL ** -0.5),
        'w_ffn_out': nrm(ks[26], (DEPTH, D_FF, D_MODEL), D_FF ** -0.5),
    }


def reference(x_prompt, x_sample, c, cache_da_k, cache_da_v, state_mlstm_C, state_mlstm_n, state_mlstm_m,
              cache_gqa_k, cache_gqa_v, c_ctx, norm1, norm2, w_mod, b_mod, w_in_even, b_gate_even,
              qk_gain_even, lam_even, da_norm_even, ml_norm_even, w_out_even, w_in_odd, qk_gain_odd,
              w_out_odd, w_ffn_in, w_ffn_out):
    P = dict(norm1=norm1, norm2=norm2, w_mod=w_mod, b_mod=b_mod, w_in_even=w_in_even,
             b_gate_even=b_gate_even, qk_gain_even=qk_gain_even, lam_even=lam_even,
             da_norm_even=da_norm_even, ml_norm_even=ml_norm_even, w_out_even=w_out_even,
             w_in_odd=w_in_odd, qk_gain_odd=qk_gain_odd, w_out_odd=w_out_odd,
             w_ffn_in=w_ffn_in, w_ffn_out=w_ffn_out)

    x = x_prompt
    cond_ctx = c_ctx[None, :]
    ctx_even, ctx_odd = [], []
    for li in range(DEPTH):
        x, new_ctx = trunk_layer(x, cond_ctx, li, P, None, None)
        if li % 2 == 0:
            ctx_even.append(new_ctx)
        else:
            ctx_odd.append(new_ctx)
    y_prompt = x

    rope = axial_rope_tables(x_sample.shape[1])
    x = x_sample
    for li in range(DEPTH):
        j = li // 2
        if li % 2 == 0:
            ctx = (cache_da_k[:, j], cache_da_v[:, j], state_mlstm_C[:, j], state_mlstm_n[:, j], state_mlstm_m[:, j])
        else:
            ctx = (cache_gqa_k[:, j], cache_gqa_v[:, j])
        x, _ = trunk_layer(x, c, li, P, rope, ctx)
    y_sample = x

    new_da_k = jnp.stack([t[0] for t in ctx_even], axis=1)
    new_da_v = jnp.stack([t[1] for t in ctx_even], axis=1)
    new_mlstm_C = jnp.stack([t[2] for t in ctx_even], axis=1)
    new_mlstm_n = jnp.stack([t[3] for t in ctx_even], axis=1)
    new_mlstm_m = jnp.stack([t[4] for t in ctx_even], axis=1)
    new_gqa_k = jnp.stack([t[0] for t in ctx_odd], axis=1)
    new_gqa_v = jnp.stack([t[1] for t in ctx_odd], axis=1)
    return (y_prompt, y_sample, new_da_k, new_da_v, new_mlstm_C, new_mlstm_n, new_mlstm_m, new_gqa_k, new_gqa_v)
```

```python
import functools
import math

import jax
import jax.numpy as jnp
from jax import lax
from jax.experimental import pallas as pl
from jax.experimental.pallas import tpu as pltpu

F32 = jnp.float32
BF16 = jnp.bfloat16
HIGHEST = lax.Precision.HIGHEST

D_MODEL = 1024
GRID_W = 64
HEAD_DIM = 64
ROPE_PAIRS = HEAD_DIM // 4
ROPE_THETA = 10000.0
EPS = 1e-6
DA_HEADS = 4
ML_HEADS = 4
ML_DIM = 128
N_GATES = 2 * 2 * ML_HEADS
GQA_Q_HEADS = 16
GQA_KV_HEADS = 4
SEG = 512
EVEN_MAIN = 7 * SEG
D_FF = -(-8 * D_MODEL // (3 * 256)) * 256

LANES = 128
VMEM_LIMIT = 56 * 2 ** 20

ML_CHUNK = 128
FF_CHUNK = 256


def _params(sem, vmem=VMEM_LIMIT):
    return pltpu.CompilerParams(dimension_semantics=sem, vmem_limit_bytes=vmem)


def _rms_rows(x, g):
    return x * lax.rsqrt(jnp.mean(x * x, axis=-1, keepdims=True) + EPS) * g


def _lane_lo():
    return lax.broadcasted_iota(jnp.int32, (1, LANES), 1) < HEAD_DIM


def _half_rms(x, gain, lo):
    x2 = x * x
    s_lo = jnp.sum(jnp.where(lo, x2, 0.0), axis=-1, keepdims=True)
    s_hi = jnp.sum(jnp.where(lo, 0.0, x2), axis=-1, keepdims=True)
    r = jnp.where(lo, lax.rsqrt(s_lo * (1.0 / HEAD_DIM) + EPS), lax.rsqrt(s_hi * (1.0 / HEAD_DIM) + EPS))
    return x * r * gain


def _rope(y, cos, sin_a, sin_b):
    return y * cos + pltpu.roll(y, LANES - ROPE_PAIRS, 1) * sin_a + pltpu.roll(y, ROPE_PAIRS, 1) * sin_b


def _sigmoid(x):
    return 1.0 / (1.0 + jnp.exp(-x))


def _mod_kernel(cond_ref, w_ref, b_ref, o_ref):
    c = cond_ref[...]
    a = c * _sigmoid(c)
    o_ref[0] = jnp.dot(a, w_ref[0], preferred_element_type=F32, precision=HIGHEST) + b_ref[0]


def _modulation(cond8, w_mod, b_mod):
    depth, _, n = w_mod.shape
    tn = 1024
    return pl.pallas_call(
        _mod_kernel,
        grid=(depth, n // tn),
        in_specs=[pl.BlockSpec((8, D_MODEL), lambda l, j: (0, 0)),
                  pl.BlockSpec((1, D_MODEL, tn), lambda l, j: (l, 0, j)),
                  pl.BlockSpec((1, 1, tn), lambda l, j: (l, 0, j))],
        out_specs=pl.BlockSpec((1, 8, tn), lambda l, j: (l, 0, j)),
        out_shape=jax.ShapeDtypeStruct((depth, 8, n), F32),
        compiler_params=_params(("arbitrary", "arbitrary")),
    )(cond8, w_mod, b_mod.reshape(depth, 1, n))


def _modulated(x_ref, mod_ref, norm_ref, which):
    sh = mod_ref[0, :, (3 * which) * D_MODEL:(3 * which + 1) * D_MODEL]
    sc = mod_ref[0, :, (3 * which + 1) * D_MODEL:(3 * which + 2) * D_MODEL]
    return _rms_rows(x_ref[...], norm_ref[...]) * (1.0 + sc) + sh


def _even_inproj_kernel(*refs, rope, cache):
    it = iter(refs)
    x_ref, mod_ref, n1_ref, w_ref, wg_ref, bg_ref, gq_ref, gk_ref = [next(it) for _ in range(8)]
    if rope:
        cos_ref, sa_ref, sb_ref = [next(it) for _ in range(3)]
    q_o, k_o, v_o, mq_o, mk_o, mv_o, mo_o, g_o, gt_o = [next(it) for _ in range(9)]
    if cache:
        kf_o, vf_o = [next(it) for _ in range(2)]

    h = _modulated(x_ref, mod_ref, n1_ref, 0)
    hb = h.astype(BF16)
    lo = _lane_lo()

    def seg(i):
        return jnp.dot(hb, w_ref[:, i * SEG:(i + 1) * SEG], preferred_element_type=F32)

    def qk_norm(p, gain_ref, hh):
        y = _half_rms(p[:, hh * LANES:(hh + 1) * LANES], gain_ref[...], lo)
        if rope:
            y = _rope(y, cos_ref[...], sa_ref[...], sb_ref[...])
        return y

    p = seg(0)
    for hh in range(DA_HEADS):
        q_o[:, hh * LANES:(hh + 1) * LANES] = (qk_norm(p, gq_ref, hh) * (HEAD_DIM ** -0.5)).astype(BF16)
    p = seg(1)
    for hh in range(DA_HEADS):
        sl = slice(hh * LANES, (hh + 1) * LANES)
        if cache:
            y = _half_rms(p[:, sl], gk_ref[...], lo)
            kf_o[:, sl] = y
        else:
            y = qk_norm(p, gk_ref, hh)
        k_o[:, sl] = y.astype(BF16)
    p = seg(2)
    v_o[...] = p.astype(BF16)
    if cache:
        vf_o[...] = p
    mq_o[...] = seg(3).astype(BF16)
    mk_o[...] = (seg(4) * (ML_DIM ** -0.5)).astype(BF16)
    mv_o[...] = seg(5).astype(BF16)
    mo_o[...] = _sigmoid(seg(6)).astype(BF16)
    pg = jnp.dot(h, wg_ref[...], preferred_element_type=F32, precision=HIGHEST) + bg_ref[...]
    g_o[...] = pg[:, :N_GATES]
    gt_o[...] = pg.T[:N_GATES, :]


def _even_inproj(x, mod, norm1, w_main, w_gate, b_gate, gain_q, gain_k, rope_tabs, rows_per_batch, cache, tm):
    r = x.shape[0]
    nt = r // tm
    tpb = rows_per_batch // tm
    row = lambda i: (i, 0)
    const = lambda i: (0, 0)
    in_specs = [pl.BlockSpec((tm, D_MODEL), row),
                pl.BlockSpec((1, 1, 6 * D_MODEL), lambda i: (i // tpb, 0, 0)),
                pl.BlockSpec((1, D_MODEL), const),
                pl.BlockSpec((D_MODEL, EVEN_MAIN), const),
                pl.BlockSpec((D_MODEL, LANES), const),
                pl.BlockSpec((1, LANES), const),
                pl.BlockSpec((1, LANES), const),
                pl.BlockSpec((1, LANES), const)]
    args = [x, mod, norm1, w_main, w_gate, b_gate, gain_q, gain_k]
    if rope_tabs is not None:
        in_specs += [pl.BlockSpec((tm, LANES), lambda i: (i % tpb, 0))] * 3
        args += list(rope_tabs)
    out_shape = [jax.ShapeDtypeStruct((r, SEG), BF16)] * 7
    out_specs = [pl.BlockSpec((tm, SEG), row)] * 7
    out_shape += [jax.ShapeDtypeStruct((r, N_GATES), F32), jax.ShapeDtypeStruct((N_GATES, r), F32)]
    out_specs += [pl.BlockSpec((tm, N_GATES), row), pl.BlockSpec((N_GATES, tm), lambda i: (0, i))]
    if cache:
        out_shape += [jax.ShapeDtypeStruct((r, SEG), F32)] * 2
        out_specs += [pl.BlockSpec((tm, SEG), row)] * 2
    return pl.pallas_call(
        functools.partial(_even_inproj_kernel, rope=rope_tabs is not None, cache=cache),
        grid=(nt,), in_specs=in_specs, out_specs=out_specs, out_shape=out_shape,
        compiler_params=_params(("arbitrary",)),
    )(*args)


def _odd_inproj_kernel(*refs, rope, cache):
    it = iter(refs)
    x_ref, mod_ref, n1_ref, w_ref, gq_ref, gk_ref = [next(it) for _ in range(6)]
    if rope:
        cos_ref, sa_ref, sb_ref = [next(it) for _ in range(3)]
    q_o, k_o, v_o = [next(it) for _ in range(3)]
    if cache:
        kf_o, vf_o = [next(it) for _ in range(2)]

    hb = _modulated(x_ref, mod_ref, n1_ref, 0).astype(BF16)
    lo = _lane_lo()
    nq = GQA_Q_HEADS * HEAD_DIM
    nkv = GQA_KV_HEADS * HEAD_DIM

    def qk_norm(t, gain_ref):
        y = _half_rms(t, gain_ref[...], lo)
        if rope:
            y = _rope(y, cos_ref[...], sa_ref[...], sb_ref[...])
        return y

    for c in range(nq // SEG):
        p = jnp.dot(hb, w_ref[:, c * SEG:(c + 1) * SEG], preferred_element_type=F32)
        for hh in range(SEG // LANES):
            sl = slice(c * SEG + hh * LANES, c * SEG + (hh + 1) * LANES)
            q_o[:, sl] = (qk_norm(p[:, hh * LANES:(hh + 1) * LANES], gq_ref) * (HEAD_DIM ** -0.5)).astype(BF16)
    p = jnp.dot(hb, w_ref[:, nq:nq + 2 * nkv], preferred_element_type=F32)
    for hh in range(nkv // LANES):
        sl = slice(hh * LANES, (hh + 1) * LANES)
        if cache:
            y = _half_rms(p[:, sl], gk_ref[...], lo)
            kf_o[:, sl] = y
        else:
            y = qk_norm(p[:, sl], gk_ref)
        k_o[:, sl] = y.astype(BF16)
    v = p[:, nkv:2 * nkv]
    v_o[...] = v.astype(BF16)
    if cache:
        vf_o[...] = v


def _odd_inproj(x, mod, norm1, w, gain_q, gain_k, rope_tabs, rows_per_batch, cache, tm):
    r = x.shape[0]
    nt = r // tm
    tpb = rows_per_batch // tm
    nq = GQA_Q_HEADS * HEAD_DIM
    nkv = GQA_KV_HEADS * HEAD_DIM
    row = lambda i: (i, 0)
    const = lambda i: (0, 0)
    in_specs = [pl.BlockSpec((tm, D_MODEL), row),
                pl.BlockSpec((1, 1, 6 * D_MODEL), lambda i: (i // tpb, 0, 0)),
                pl.BlockSpec((1, D_MODEL), const),
                pl.BlockSpec((D_MODEL, nq + 2 * nkv), const),
                pl.BlockSpec((1, LANES), const),
                pl.BlockSpec((1, LANES), const)]
    args = [x, mod, norm1, w, gain_q, gain_k]
    if rope_tabs is not None:
        in_specs += [pl.BlockSpec((tm, LANES), lambda i: (i % tpb, 0))] * 3
        args += list(rope_tabs)
    out_shape = [jax.ShapeDtypeStruct((r, nq), BF16), jax.ShapeDtypeStruct((r, nkv), BF16),
                 jax.ShapeDtypeStruct((r, nkv), BF16)]
    out_specs = [pl.BlockSpec((tm, nq), row), pl.BlockSpec((tm, nkv), row), pl.BlockSpec((tm, nkv), row)]
    if cache:
        out_shape += [jax.ShapeDtypeStruct((r, nkv), F32)] * 2
        out_specs += [pl.BlockSpec((tm, nkv), row)] * 2
    return pl.pallas_call(
        functools.partial(_odd_inproj_kernel, rope=rope_tabs is not None, cache=cache),
        grid=(nt,), in_specs=in_specs, out_specs=out_specs, out_shape=out_shape,
        compiler_params=_params(("arbitrary",)),
    )(*args)


_NT = (((1,), (1,)), ((), ()))


def _attend(q, sources, tk):
    tq = q.shape[0]

    def step(carry, kc, vc):
        s = lax.dot_general(q, kc, _NT, preferred_element_type=F32)
        s_max = jnp.max(s, axis=-1, keepdims=True)
        if carry is None:
            m_new = s_max
            p = jnp.exp(s - m_new)
            l_new = jnp.sum(p, axis=-1, keepdims=True)
            acc_new = jnp.dot(p.astype(BF16), vc, preferred_element_type=F32)
        else:
            m, l, acc = carry
            m_new = jnp.maximum(m, s_max)
            alpha = jnp.exp(m - m_new)
            p = jnp.exp(s - m_new)
            l_new = alpha * l + jnp.sum(p, axis=-1, keepdims=True)
            acc_new = alpha * acc + jnp.dot(p.astype(BF16), vc, preferred_element_type=F32)
        return m_new, l_new, acc_new

    carry = None
    for k_ref, v_ref in sources:
        n = k_ref.shape[0] // tk
        start = 0
        if carry is None:
            carry = step(None, k_ref[0:tk, :], v_ref[0:tk, :])
            start = 1
        if n - start > 2:
            def body(c, cr, k_ref=k_ref, v_ref=v_ref):
                off = pl.multiple_of(c * tk, tk)
                return step(cr, k_ref[pl.ds(off, tk), :], v_ref[pl.ds(off, tk), :])
            carry = lax.fori_loop(start, n, body, carry)
        else:
            for c in range(start, n):
                carry = step(carry, k_ref[c * tk:(c + 1) * tk, :], v_ref[c * tk:(c + 1) * tk, :])
    m, l, acc = carry
    return acc / l


def _da_attn_kernel(*refs, n_src, tk, lam_init):
    q_ref, lam_ref, gn_ref = refs[:3]
    kv = refs[3:3 + 2 * n_src]
    o_ref = refs[3 + 2 * n_src]
    sources = [(kv[2 * i], kv[2 * i + 1]) for i in range(n_src)]
    q = q_ref[...]
    lo = _lane_lo()
    zero = jnp.zeros_like(q)
    a1 = _attend(jnp.where(lo, q, zero), sources, tk)
    a2 = _attend(jnp.where(lo, zero, q), sources, tk)
    lam = lam_ref[...]
    lam_full = (jnp.exp(jnp.sum(lam[0:1] * lam[1:2], axis=-1, keepdims=True))
                - jnp.exp(jnp.sum(lam[2:3] * lam[3:4], axis=-1, keepdims=True)) + lam_init)
    o = a1 - lam_full * a2
    o_ref[...] = (_rms_rows(o, gn_ref[...]) * (1.0 - lam_init)).astype(BF16)


def _da_attention(q, k, v, ck, cv, lam, gain, batch, lam_init, tq, tk):
    r = q.shape[0]
    s = r // batch
    nq = s // tq
    qmap = lambda b, h, i: (b * nq + i, h)
    kvmap = lambda b, h, i: (b, h)
    in_specs = [pl.BlockSpec((tq, LANES), qmap),
                pl.BlockSpec(lam.shape, lambda b, h, i: (0, 0)),
                pl.BlockSpec((1, LANES), lambda b, h, i: (0, 0)),
                pl.BlockSpec((s, LANES), kvmap), pl.BlockSpec((s, LANES), kvmap)]
    args = [q, lam, gain, k, v]
    n_src = 1
    if ck is not None:
        past = ck.shape[0] // batch
        in_specs += [pl.BlockSpec((past, LANES), kvmap)] * 2
        args += [ck, cv]
        n_src = 2
    return pl.pallas_call(
        functools.partial(_da_attn_kernel, n_src=n_src, tk=tk, lam_init=lam_init),
        grid=(batch, DA_HEADS, nq), in_specs=in_specs,
        out_specs=pl.BlockSpec((tq, LANES), qmap),
        out_shape=jax.ShapeDtypeStruct((r, SEG), BF16),
        compiler_params=_params(("arbitrary", "arbitrary", "arbitrary")),
    )(*args)


def _gqa_attn_kernel(*refs, n_src, tk):
    q_ref = refs[0]
    kv = refs[1:1 + 2 * n_src]
    o_ref = refs[1 + 2 * n_src]
    sources = [(kv[2 * i], kv[2 * i + 1]) for i in range(n_src)]
    lo = _lane_lo()
    group = GQA_Q_HEADS // GQA_KV_HEADS
    for half in range(2):
        keep = lo if half == 0 else jnp.logical_not(lo)
        acc = []
        for hh in range(group):
            head = half * group + hh
            qp = q_ref[:, (head // 2) * LANES:(head // 2 + 1) * LANES].astype(F32)
            if head % 2 != half:
                qp = pltpu.roll(qp, HEAD_DIM, 1)
            acc.append(_attend(jnp.where(keep, qp, 0.0).astype(BF16), sources, tk))
        for pr in range(group // 2):
            a, b = acc[2 * pr], acc[2 * pr + 1]
            if half == 0:
                o = jnp.where(lo, a, pltpu.roll(b, HEAD_DIM, 1))
            else:
                o = jnp.where(lo, pltpu.roll(a, HEAD_DIM, 1), b)
            col = (half * group // 2 + pr) * LANES
            o_ref[:, col:col + LANES] = o.astype(BF16)


def _gqa_attention(q, k, v, ck, cv, batch, tq, tk):
    r = q.shape[0]
    s = r // batch
    nq = s // tq
    pairs = GQA_KV_HEADS // 2
    qw = q.shape[1] // pairs
    qmap = lambda b, j, i: (b * nq + i, j)
    kvmap = lambda b, j, i: (b, j)
    in_specs = [pl.BlockSpec((tq, qw), qmap), pl.BlockSpec((s, LANES), kvmap), pl.BlockSpec((s, LANES), kvmap)]
    args = [q, k, v]
    n_src = 1
    if ck is not None:
        past = ck.shape[0] // batch
        in_specs += [pl.BlockSpec((past, LANES), kvmap)] * 2
        args += [ck, cv]
        n_src = 2
    return pl.pallas_call(
        functools.partial(_gqa_attn_kernel, n_src=n_src, tk=tk),
        grid=(batch, pairs, nq), in_specs=in_specs,
        out_specs=pl.BlockSpec((tq, qw), qmap),
        out_shape=jax.ShapeDtypeStruct(q.shape, BF16),
        compiler_params=_params(("arbitrary", "arbitrary", "arbitrary")),
    )(*args)


def _log_sigmoid(x):
    return jnp.minimum(x, 0.0) - jnp.log1p(jnp.exp(-jnp.abs(x)))


def _mlstm_chunk(q_ref, k_ref, v_ref, g_ref, gt_ref, c_s, n_s, m_s, h_ref, head, direction):
    L = ML_CHUNK
    q = q_ref[...]
    k = k_ref[...]
    v = v_ref[...]
    i_idx = direction * 2 * ML_HEADS + head
    f_idx = i_idx + ML_HEADS
    g = g_ref[...]
    gt = gt_ref[...]
    lane = lax.broadcasted_iota(jnp.int32, (1, N_GATES), 1)
    sub = lax.broadcasted_iota(jnp.int32, (N_GATES, 1), 0)
    i_col = jnp.sum(jnp.where(lane == i_idx, g, 0.0), axis=1, keepdims=True)
    f_col = jnp.sum(jnp.where(lane == f_idx, g, 0.0), axis=1, keepdims=True)
    i_row = jnp.sum(jnp.where(sub == i_idx, gt, 0.0), axis=0, keepdims=True)
    f_row = jnp.sum(jnp.where(sub == f_idx, gt, 0.0), axis=0, keepdims=True)
    lf_col = _log_sigmoid(f_col)
    lf_row = _log_sigmoid(f_row)
    t_i = lax.broadcasted_iota(jnp.int32, (L, L), 0)
    s_i = lax.broadcasted_iota(jnp.int32, (L, L), 1)
    vis = (s_i <= t_i) if direction == 0 else (s_i >= t_i)
    vis_t = (t_i <= s_i) if direction == 0 else (t_i >= s_i)
    b_col = jnp.sum(jnp.where(vis, lf_row, 0.0), axis=1, keepdims=True)
    b_row = jnp.sum(jnp.where(vis_t, lf_col, 0.0), axis=0, keepdims=True)
    b_last = jnp.sum(lf_row, axis=1, keepdims=True)
    m_prev = m_s[:, 0:1]
    dmat = jnp.where(vis, b_col - b_row + i_row, -jnp.inf)
    inter = b_col + m_prev
    m_t = jnp.maximum(inter, jnp.max(dmat, axis=1, keepdims=True))
    w_intra = jnp.exp(dmat - m_t)
    w_inter = jnp.exp(inter - m_t)
    qk = lax.dot_general(q, k, _NT, preferred_element_type=F32) * w_intra
    c_old = c_s[...]
    n_old = n_s[...]
    qf = q.astype(F32)
    kf = k.astype(F32)
    num = (jnp.dot(qk.astype(BF16), v, preferred_element_type=F32)
           + w_inter * lax.dot_general(q, c_old.astype(BF16), _NT, preferred_element_type=F32))
    den = jnp.sum(qk, axis=1, keepdims=True) + w_inter * jnp.sum(qf * n_old, axis=1, keepdims=True)
    h_ref[...] = num / jnp.maximum(jnp.abs(den), jnp.exp(-m_t))
    m_new = m_t[L - 1:L, :] if direction == 0 else m_t[0:1, :]
    w_s = jnp.exp(b_last - b_col + i_col - m_new)
    decay = jnp.exp(b_last + m_prev - m_new)
    vw = (v.astype(F32) * w_s).T.astype(BF16)
    c_s[...] = decay * c_old + jnp.dot(vw, k, preferred_element_type=F32)
    n_s[...] = decay * n_old + jnp.sum(kf * w_s, axis=0, keepdims=True)
    m_s[...] = jnp.broadcast_to(m_new, m_s.shape)


def _mlstm_kernel(*refs, has_init):
    it = iter(refs)
    fwd = [next(it) for _ in range(5)]
    bwd = [next(it) for _ in range(5)]
    if has_init:
        c0f, c0b, n0f, n0b, m0f, m0b = [next(it) for _ in range(6)]
    hf_o, hb_o, c_o, n_o, m_o = [next(it) for _ in range(5)]
    c_f, c_b, n_f, n_b, m_f, m_b = [next(it) for _ in range(6)]
    head = pl.program_id(1)
    chunk = pl.program_id(2)

    @pl.when(chunk == 0)
    def _():
        if has_init:
            c_f[...] = c0f[0, 0, 0]
            c_b[...] = c0b[0, 0, 0]
            n_f[...] = n0f[0, 0, 0]
            n_b[...] = n0b[0, 0, 0]
            m_f[...] = m0f[0, 0, 0]
            m_b[...] = m0b[0, 0, 0]
        else:
            for ref in (c_f, c_b, n_f, n_b, m_f, m_b):
                ref[...] = jnp.zeros(ref.shape, F32)

    _mlstm_chunk(*fwd, c_f, n_f, m_f, hf_o, head, 0)
    _mlstm_chunk(*bwd, c_b, n_b, m_b, hb_o, head, 1)

    @pl.when(chunk == pl.num_programs(2) - 1)
    def _():
        c_o[0, 0, 0] = c_f[...]
        c_o[0, 1, 0] = c_b[...]
        n_o[0, 0, 0] = n_f[...]
        n_o[0, 1, 0] = n_b[...]
        m_o[0, 0, 0] = m_f[...]
        m_o[0, 1, 0] = m_b[...]


def _mlstm(q, k, v, g, gt, init, batch):
    r = q.shape[0]
    L = ML_CHUNK
    nc = r // batch // L
    fmap = lambda b, h, c: (b * nc + c, h)
    bmap = lambda b, h, c: (b * nc + nc - 1 - c, h)
    fg = lambda b, h, c: (b * nc + c, 0)
    bg = lambda b, h, c: (b * nc + nc - 1 - c, 0)
    fgt = lambda b, h, c: (0, b * nc + c)
    bgt = lambda b, h, c: (0, b * nc + nc - 1 - c)

    def side(m, mg, mgt):
        return [pl.BlockSpec((L, ML_DIM), m)] * 3 + [pl.BlockSpec((L, N_GATES), mg), pl.BlockSpec((N_GATES, L), mgt)]

    in_specs = side(fmap, fg, fgt) + side(bmap, bg, bgt)
    args = [q, k, v, g, gt] * 2
    if init is not None:
        c0, n0, m0 = init
        for arr, shp in ((c0, (ML_DIM, ML_DIM)), (n0, (1, ML_DIM)), (m0, (1, ML_DIM))):
            for d in range(2):
                in_specs.append(pl.BlockSpec((1, 1, 1) + shp, lambda b, h, c, d=d: (b, d, h, 0, 0)))
                args.append(arr)
    state = lambda b, h, c: (b, 0, h, 0, 0)
    out_shape = [jax.ShapeDtypeStruct((r, ML_HEADS * ML_DIM), F32)] * 2 + [
        jax.ShapeDtypeStruct((batch, 2, ML_HEADS, ML_DIM, ML_DIM), F32),
        jax.ShapeDtypeStruct((batch, 2, ML_HEADS, 1, ML_DIM), F32),
        jax.ShapeDtypeStruct((batch, 2, ML_HEADS, 1, ML_DIM), F32)]
    out_specs = [pl.BlockSpec((L, ML_DIM), fmap), pl.BlockSpec((L, ML_DIM), bmap),
                 pl.BlockSpec((1, 2, 1, ML_DIM, ML_DIM), state),
                 pl.BlockSpec((1, 2, 1, 1, ML_DIM), state),
                 pl.BlockSpec((1, 2, 1, 1, ML_DIM), state)]
    scratch = [pltpu.VMEM((ML_DIM, ML_DIM), F32)] * 2 + [pltpu.VMEM((1, ML_DIM), F32)] * 4
    return pl.pallas_call(
        functools.partial(_mlstm_kernel, has_init=init is not None),
        grid=(batch, ML_HEADS, nc), in_specs=in_specs, out_specs=out_specs, out_shape=out_shape,
        scratch_shapes=scratch,
        compiler_params=_params(("arbitrary", "arbitrary", "arbitrary")),
    )(*args)


def _gate(mod_ref, which):
    return mod_ref[0, :, (3 * which + 2) * D_MODEL:(3 * which + 3) * D_MODEL]


def _even_outproj_kernel(da_ref, hf_ref, hb_ref, mo_ref, gn_ref, w_ref, x_ref, mod_ref, o_ref):
    acc = jnp.dot(da_ref[...], w_ref[0:SEG, :], preferred_element_type=F32)
    for hh in range(ML_HEADS):
        sl = slice(hh * ML_DIM, (hh + 1) * ML_DIM)
        y = _rms_rows(hf_ref[:, sl] + hb_ref[:, sl], gn_ref[...]) * mo_ref[:, sl].astype(F32)
        acc += jnp.dot(y.astype(BF16), w_ref[SEG + hh * ML_DIM:SEG + (hh + 1) * ML_DIM, :],
                       preferred_element_type=F32)
    o_ref[...] = x_ref[...] + _gate(mod_ref, 0) * acc


def _even_outproj(da, hf, hb, mo, gain, w, x, mod, rows_per_batch, tm):
    r = x.shape[0]
    tpb = rows_per_batch // tm
    row = lambda i: (i, 0)
    const = lambda i: (0, 0)
    return pl.pallas_call(
        _even_outproj_kernel, grid=(r // tm,),
        in_specs=[pl.BlockSpec((tm, SEG), row)] * 4 + [
            pl.BlockSpec((1, ML_DIM), const), pl.BlockSpec(w.shape, const),
            pl.BlockSpec((tm, D_MODEL), row), pl.BlockSpec((1, 1, 6 * D_MODEL), lambda i: (i // tpb, 0, 0))],
        out_specs=pl.BlockSpec((tm, D_MODEL), row),
        out_shape=jax.ShapeDtypeStruct(x.shape, F32),
        compiler_params=_params(("arbitrary",)),
    )(da, hf, hb, mo, gain, w, x, mod)


def _odd_outproj_kernel(a_ref, w_ref, x_ref, mod_ref, o_ref):
    acc = jnp.dot(a_ref[...], w_ref[...], preferred_element_type=F32)
    o_ref[...] = x_ref[...] + _gate(mod_ref, 0) * acc


def _odd_outproj(a, w, x, mod, rows_per_batch, tm):
    r = x.shape[0]
    tpb = rows_per_batch // tm
    row = lambda i: (i, 0)
    return pl.pallas_call(
        _odd_outproj_kernel, grid=(r // tm,),
        in_specs=[pl.BlockSpec((tm, a.shape[1]), row), pl.BlockSpec(w.shape, lambda i: (0, 0)),
                  pl.BlockSpec((tm, D_MODEL), row), pl.BlockSpec((1, 1, 6 * D_MODEL), lambda i: (i // tpb, 0, 0))],
        out_specs=pl.BlockSpec((tm, D_MODEL), row),
        out_shape=jax.ShapeDtypeStruct(x.shape, F32),
        compiler_params=_params(("arbitrary",)),
    )(a, w, x, mod)


def _ffn_kernel(x_ref, mod_ref, n2_ref, wi_ref, wo_ref, o_ref):
    hb = _modulated(x_ref, mod_ref, n2_ref, 1).astype(BF16)
    acc = jnp.zeros(o_ref.shape, F32)
    for c in range(D_FF // FF_CHUNK):
        sl = slice(c * FF_CHUNK, (c + 1) * FF_CHUNK)
        gate = jnp.dot(hb, wi_ref[:, sl], preferred_element_type=F32)
        up = jnp.dot(hb, wi_ref[:, D_FF + c * FF_CHUNK:D_FF + (c + 1) * FF_CHUNK], preferred_element_type=F32)
        act = (gate * _sigmoid(gate) * up).astype(BF16)
        acc += jnp.dot(act, wo_ref[sl, :], preferred_element_type=F32)
    o_ref[...] = x_ref[...] + _gate(mod_ref, 1) * acc


def _ffn(x, mod, norm2, w_in, w_out, rows_per_batch, tm):
    r = x.shape[0]
    tpb = rows_per_batch // tm
    row = lambda i: (i, 0)
    const = lambda i: (0, 0)
    return pl.pallas_call(
        _ffn_kernel, grid=(r // tm,),
        in_specs=[pl.BlockSpec((tm, D_MODEL), row), pl.BlockSpec((1, 1, 6 * D_MODEL), lambda i: (i // tpb, 0, 0)),
                  pl.BlockSpec((1, D_MODEL), const), pl.BlockSpec(w_in.shape, const), pl.BlockSpec(w_out.shape, const)],
        out_specs=pl.BlockSpec((tm, D_MODEL), row),
        out_shape=jax.ShapeDtypeStruct(x.shape, F32),
        compiler_params=_params(("arbitrary",)),
    )(x, mod, norm2, w_in, w_out)


def _rope_tables(n_tokens):
    t = jnp.arange(n_tokens)
    pos = jnp.stack([(t // GRID_W).astype(F32), (t % GRID_W).astype(F32)], axis=1)
    freqs = ROPE_THETA ** (-jnp.arange(ROPE_PAIRS, dtype=F32) / ROPE_PAIRS)
    ang = pos[:, :, None] * freqs
    cos, sin = jnp.cos(ang), jnp.sin(ang)
    zero = jnp.zeros_like(sin)
    lay = lambda first, second: jnp.tile(jnp.stack([first, second], axis=2).reshape(n_tokens, HEAD_DIM), (1, 2))
    return lay(cos, cos), lay(-sin, zero), lay(zero, sin)


def _trunk(x, mods, li, P, batch, rope_tabs, ctx, tm):
    lj = li // 2
    rows_per_batch = x.shape[0] // batch
    mod_rpb = rows_per_batch if mods.shape[0] > 1 else x.shape[0]
    is_ctx = ctx is None
    new = None
    if li % 2 == 0:
        lam_init = 0.8 - 0.6 * math.exp(-0.3 * li)
        outs = _even_inproj(x, mods, P['norm1'][li], P['w_even_main'][lj], P['w_even_gate'][lj], P['b_gate'][lj],
                            P['gq_even'][lj], P['gk_even'][lj], rope_tabs, mod_rpb, is_ctx, tm)
        q, k, v, mq, mk, mv, mo, g, gt = outs[:9]
        if is_ctx:
            ck = cv = init = None
            tq, tk = 256, 256
        else:
            ck, cv, init = ctx
            tq, tk = 256, 512
        da = _da_attention(q, k, v, ck, cv, P['lam_even'][lj], P['da_norm'][lj], batch, lam_init, tq, tk)
        hf, hb, c_n, n_n, m_n = _mlstm(mq, mk, mv, g, gt, init, batch)
        x = _even_outproj(da, hf, hb, mo, P['ml_norm'][lj], P['w_out_even'][lj], x, mods, mod_rpb, tm)
        if is_ctx:
            new = (outs[9], outs[10], c_n, n_n, m_n)
    else:
        outs = _odd_inproj(x, mods, P['norm1'][li], P['w_in_odd'][lj], P['gq_odd'][lj], P['gk_odd'][lj],
                           rope_tabs, mod_rpb, is_ctx, tm)
        q, k, v = outs[:3]
        if is_ctx:
            ck = cv = None
            tq, tk = 256, 256
        else:
            ck, cv = ctx
            tq, tk = 256, 512
        a = _gqa_attention(q, k, v, ck, cv, batch, tq, tk)
        x = _odd_outproj(a, P['w_out_odd'][lj], x, mods, mod_rpb, tm)
        if is_ctx:
            new = (outs[3], outs[4])
    x = _ffn(x, mods, P['norm2'][li], P['w_ffn_in'][li], P['w_ffn_out'][li], mod_rpb, tm)
    return x, new


def kernel(x_prompt, x_sample, c, cache_da_k, cache_da_v, state_mlstm_C, state_mlstm_n, state_mlstm_m, cache_gqa_k, cache_gqa_v, c_ctx, norm1, norm2, w_mod, b_mod, w_in_even, b_gate_even, qk_gain_even, lam_even, da_norm_even, ml_norm_even, w_out_even, w_in_odd, qk_gain_odd, w_out_odd, w_ffn_in, w_ffn_out):
    batch, seq, _ = x_prompt.shape
    dbatch, dseq, _ = x_sample.shape
    depth = norm1.shape[0]
    n_even, n_odd = w_in_even.shape[0], w_in_odd.shape[0]
    past = cache_da_k.shape[2]
    tm = 512

    P = dict(
        norm1=norm1.reshape(depth, 1, D_MODEL), norm2=norm2.reshape(depth, 1, D_MODEL),
        w_even_main=w_in_even[:, :, :EVEN_MAIN].astype(BF16),
        w_even_gate=jnp.pad(w_in_even[:, :, EVEN_MAIN:], ((0, 0), (0, 0), (0, LANES - N_GATES))),
        b_gate=jnp.pad(b_gate_even.reshape(n_even, 1, N_GATES), ((0, 0), (0, 0), (0, LANES - N_GATES))),
        gq_even=qk_gain_even[:, 0].reshape(n_even, 1, LANES), gk_even=qk_gain_even[:, 1].reshape(n_even, 1, LANES),
        lam_even=lam_even, da_norm=da_norm_even.reshape(n_even, 1, LANES),
        ml_norm=ml_norm_even.reshape(n_even, 1, ML_DIM),
        w_out_even=w_out_even.astype(BF16), w_in_odd=w_in_odd.astype(BF16),
        gq_odd=jnp.tile(qk_gain_odd[:, 0], (1, 2)).reshape(n_odd, 1, LANES),
        gk_odd=jnp.tile(qk_gain_odd[:, 1], (1, 2)).reshape(n_odd, 1, LANES),
        w_out_odd=w_out_odd.astype(BF16), w_ffn_in=w_ffn_in.astype(BF16), w_ffn_out=w_ffn_out.astype(BF16))

    cond8 = jnp.concatenate([c_ctx[None, :], c, jnp.zeros((8 - 1 - dbatch, D_MODEL), F32)], axis=0)
    mod = _modulation(cond8, w_mod, b_mod)
    rope_tabs = _rope_tables(dseq)

    x = x_prompt.reshape(batch * seq, D_MODEL)
    ctx_even, ctx_odd = [], []
    for li in range(depth):
        x, new = _trunk(x, mod[li, 0:1].reshape(1, 1, 6 * D_MODEL), li, P, batch, None, None, tm)
        (ctx_even if li % 2 == 0 else ctx_odd).append(new)
    y_prompt = x.reshape(batch, seq, D_MODEL)

    x = x_sample.reshape(dbatch * dseq, D_MODEL)
    for li in range(depth):
        j = li // 2
        if li % 2 == 0:
            ctx = (cache_da_k[:, j].reshape(dbatch * past, SEG).astype(BF16),
                   cache_da_v[:, j].reshape(dbatch * past, SEG).astype(BF16),
                   (state_mlstm_C[:, j], state_mlstm_n[:, j][:, :, :, None, :],
                    jnp.broadcast_to(state_mlstm_m[:, j][:, :, :, None, None], (dbatch, 2, ML_HEADS, 1, ML_DIM))))
        else:
            ctx = (cache_gqa_k[:, j].reshape(dbatch * past, GQA_KV_HEADS * HEAD_DIM).astype(BF16),
                   cache_gqa_v[:, j].reshape(dbatch * past, GQA_KV_HEADS * HEAD_DIM).astype(BF16))
        x, _ = _trunk(x, mod[li, 1:1 + dbatch].reshape(dbatch, 1, 6 * D_MODEL), li, P, dbatch, rope_tabs, ctx, tm)
    y_sample = x.reshape(dbatch, dseq, D_MODEL)

    new_da_k = jnp.stack([t[0].reshape(batch, seq, DA_HEADS, 2, HEAD_DIM) for t in ctx_even], axis=1)
    new_da_v = jnp.stack([t[1].reshape(batch, seq, DA_HEADS, 2 * HEAD_DIM) for t in ctx_even], axis=1)
    new_c = jnp.stack([t[2] for t in ctx_even], axis=1)
    new_n = jnp.stack([t[3][:, :, :, 0, :] for t in ctx_even], axis=1)
    new_m = jnp.stack([t[4][:, :, :, 0, 0] for t in ctx_even], axis=1)
    new_gqa_k = jnp.stack([t[0].reshape(batch, seq, GQA_KV_HEADS, HEAD_DIM) for t in ctx_odd], axis=1)
    new_gqa_v = jnp.stack([t[1].reshape(batch, seq, GQA_KV_HEADS, HEAD_DIM) for t in ctx_odd], axis=1)
    return (y_prompt, y_sample, new_da_k, new_da_v, new_c, new_n, new_m, new_gqa_k, new_gqa_v)
```

```python
import functools
import math

import jax
import jax.numpy as jnp
from jax import lax
from jax.experimental import pallas as pl
from jax.experimental.pallas import tpu as pltpu

F32 = jnp.float32
BF16 = jnp.bfloat16
HIGHEST = lax.Precision.HIGHEST

D_MODEL = 1024
GRID_W = 64
HEAD_DIM = 64
ROPE_PAIRS = HEAD_DIM // 4
ROPE_THETA = 10000.0
EPS = 1e-6
DA_HEADS = 4
ML_HEADS = 4
ML_DIM = 128
N_GATES = 2 * 2 * ML_HEADS
GQA_Q_HEADS = 16
GQA_KV_HEADS = 4
SEG = 512
EVEN_MAIN = 7 * SEG
D_FF = -(-8 * D_MODEL // (3 * 256)) * 256

LANES = 128
VMEM_LIMIT = 56 * 2 ** 20

ATTN_TQ = 256
Q_SCALE = math.log2(math.e) * HEAD_DIM ** -0.5
ML_CHUNK = 128
FF_CHUNK = 256


def _params(sem, vmem=VMEM_LIMIT):
    return pltpu.CompilerParams(dimension_semantics=sem, vmem_limit_bytes=vmem)


def _rms_rows(x, g):
    return x * lax.rsqrt(jnp.mean(x * x, axis=-1, keepdims=True) + EPS) * g


def _lane_lo():
    return lax.broadcasted_iota(jnp.int32, (1, LANES), 1) < HEAD_DIM


def _half_rms(x, gain, lo):
    x2 = x * x
    s_lo = jnp.sum(jnp.where(lo, x2, 0.0), axis=-1, keepdims=True)
    s_hi = jnp.sum(jnp.where(lo, 0.0, x2), axis=-1, keepdims=True)
    r = jnp.where(lo, lax.rsqrt(s_lo * (1.0 / HEAD_DIM) + EPS), lax.rsqrt(s_hi * (1.0 / HEAD_DIM) + EPS))
    return x * r * gain


def _rope(y, cos, sin_a, sin_b):
    return y * cos + pltpu.roll(y, LANES - ROPE_PAIRS, 1) * sin_a + pltpu.roll(y, ROPE_PAIRS, 1) * sin_b


def _sigmoid(x):
    return 1.0 / (1.0 + jnp.exp(-x))


def _mod_kernel(cond_ref, w_ref, b_ref, o_ref):
    c = cond_ref[...]
    a = c * _sigmoid(c)
    o_ref[0] = jnp.dot(a, w_ref[0], preferred_element_type=F32, precision=HIGHEST) + b_ref[0]


def _modulation(cond8, w_mod, b_mod):
    depth, _, n = w_mod.shape
    tn = 1024
    return pl.pallas_call(
        _mod_kernel,
        grid=(depth, n // tn),
        in_specs=[pl.BlockSpec((8, D_MODEL), lambda l, j: (0, 0)),
                  pl.BlockSpec((1, D_MODEL, tn), lambda l, j: (l, 0, j)),
                  pl.BlockSpec((1, 1, tn), lambda l, j: (l, 0, j))],
        out_specs=pl.BlockSpec((1, 8, tn), lambda l, j: (l, 0, j)),
        out_shape=jax.ShapeDtypeStruct((depth, 8, n), F32),
        compiler_params=_params(("arbitrary", "arbitrary")), name="modulation",
    )(cond8, w_mod, b_mod.reshape(depth, 1, n))


def _modulated(x_ref, mod_ref, norm_ref, which):
    sh = mod_ref[0, :, (3 * which) * D_MODEL:(3 * which + 1) * D_MODEL]
    sc = mod_ref[0, :, (3 * which + 1) * D_MODEL:(3 * which + 2) * D_MODEL]
    return _rms_rows(x_ref[...], norm_ref[...]) * (1.0 + sc) + sh


def _even_inproj_kernel(*refs, rope, cache):
    it = iter(refs)
    x_ref, mod_ref, n1_ref, w_ref, wg_ref, bg_ref, gq_ref, gk_ref = [next(it) for _ in range(8)]
    if rope:
        cos_ref, sa_ref, sb_ref = [next(it) for _ in range(3)]
    q_o, k_o, v_o, mq_o, mk_o, mv_o, mo_o, g_o, gt_o = [next(it) for _ in range(9)]
    if cache:
        kf_o, vf_o = [next(it) for _ in range(2)]

    h = _modulated(x_ref, mod_ref, n1_ref, 0)
    hb = h.astype(BF16)
    lo = _lane_lo()

    def seg(i):
        return jnp.dot(hb, w_ref[:, i * SEG:(i + 1) * SEG], preferred_element_type=F32)

    def qk_norm(p, gain_ref, hh):
        y = _half_rms(p[:, hh * LANES:(hh + 1) * LANES], gain_ref[...], lo)
        if rope:
            y = _rope(y, cos_ref[...], sa_ref[...], sb_ref[...])
        return y

    p = seg(0)
    for hh in range(DA_HEADS):
        q_o[:, hh * LANES:(hh + 1) * LANES] = (qk_norm(p, gq_ref, hh) * Q_SCALE).astype(BF16)
    p = seg(1)
    for hh in range(DA_HEADS):
        sl = slice(hh * LANES, (hh + 1) * LANES)
        if cache:
            y = _half_rms(p[:, sl], gk_ref[...], lo)
            kf_o[:, sl] = y
        else:
            y = qk_norm(p, gk_ref, hh)
        k_o[:, sl] = y.astype(BF16)
    p = seg(2)
    v_o[...] = p.astype(BF16)
    if cache:
        vf_o[...] = p
    mq_o[...] = seg(3).astype(BF16)
    mk_o[...] = (seg(4) * (ML_DIM ** -0.5)).astype(BF16)
    mv_o[...] = seg(5).astype(BF16)
    mo_o[...] = _sigmoid(seg(6)).astype(BF16)
    pg = jnp.dot(h, wg_ref[...], preferred_element_type=F32, precision=HIGHEST) + bg_ref[...]
    g_o[...] = pg[:, :N_GATES]
    gt_o[...] = pg.T[:N_GATES, :]


def _even_inproj(x, mod, norm1, w_main, w_gate, b_gate, gain_q, gain_k, rope_tabs, rows_per_batch, cache, tm):
    r = x.shape[0]
    nt = r // tm
    tpb = rows_per_batch // tm
    row = lambda i: (i, 0)
    const = lambda i: (0, 0)
    in_specs = [pl.BlockSpec((tm, D_MODEL), row),
                pl.BlockSpec((1, 1, 6 * D_MODEL), lambda i: (i // tpb, 0, 0)),
                pl.BlockSpec((1, D_MODEL), const),
                pl.BlockSpec((D_MODEL, EVEN_MAIN), const),
                pl.BlockSpec((D_MODEL, LANES), const),
                pl.BlockSpec((1, LANES), const),
                pl.BlockSpec((1, LANES), const),
                pl.BlockSpec((1, LANES), const)]
    args = [x, mod, norm1, w_main, w_gate, b_gate, gain_q, gain_k]
    if rope_tabs is not None:
        in_specs += [pl.BlockSpec((tm, LANES), lambda i: (i % tpb, 0))] * 3
        args += list(rope_tabs)
    out_shape = [jax.ShapeDtypeStruct((r, SEG), BF16)] * 7
    out_specs = [pl.BlockSpec((tm, SEG), row)] * 7
    out_shape += [jax.ShapeDtypeStruct((r, N_GATES), F32), jax.ShapeDtypeStruct((N_GATES, r), F32)]
    out_specs += [pl.BlockSpec((tm, N_GATES), row), pl.BlockSpec((N_GATES, tm), lambda i: (0, i))]
    if cache:
        out_shape += [jax.ShapeDtypeStruct((r, SEG), F32)] * 2
        out_specs += [pl.BlockSpec((tm, SEG), row)] * 2
    return pl.pallas_call(
        functools.partial(_even_inproj_kernel, rope=rope_tabs is not None, cache=cache),
        grid=(nt,), in_specs=in_specs, out_specs=out_specs, out_shape=out_shape,
        compiler_params=_params(("arbitrary",)), name="even_inproj_ctx" if cache else "even_inproj_dec",
    )(*args)


def _odd_inproj_kernel(*refs, rope, cache):
    it = iter(refs)
    x_ref, mod_ref, n1_ref, w_ref, gq_ref, gk_ref = [next(it) for _ in range(6)]
    if rope:
        cos_ref, sa_ref, sb_ref = [next(it) for _ in range(3)]
    q_o, k_o, v_o = [next(it) for _ in range(3)]
    if cache:
        kf_o, vf_o = [next(it) for _ in range(2)]

    hb = _modulated(x_ref, mod_ref, n1_ref, 0).astype(BF16)
    lo = _lane_lo()
    nq = GQA_Q_HEADS * HEAD_DIM
    nkv = GQA_KV_HEADS * HEAD_DIM

    def qk_norm(t, gain_ref):
        y = _half_rms(t, gain_ref[...], lo)
        if rope:
            y = _rope(y, cos_ref[...], sa_ref[...], sb_ref[...])
        return y

    for c in range(nq // SEG):
        p = jnp.dot(hb, w_ref[:, c * SEG:(c + 1) * SEG], preferred_element_type=F32)
        for hh in range(SEG // LANES):
            sl = slice(c * SEG + hh * LANES, c * SEG + (hh + 1) * LANES)
            q_o[:, sl] = (qk_norm(p[:, hh * LANES:(hh + 1) * LANES], gq_ref) * Q_SCALE).astype(BF16)
    p = jnp.dot(hb, w_ref[:, nq:nq + 2 * nkv], preferred_element_type=F32)
    for hh in range(nkv // LANES):
        sl = slice(hh * LANES, (hh + 1) * LANES)
        if cache:
            y = _half_rms(p[:, sl], gk_ref[...], lo)
            kf_o[:, sl] = y
        else:
            y = qk_norm(p[:, sl], gk_ref)
        k_o[:, sl] = y.astype(BF16)
    v = p[:, nkv:2 * nkv]
    v_o[...] = v.astype(BF16)
    if cache:
        vf_o[...] = v


def _odd_inproj(x, mod, norm1, w, gain_q, gain_k, rope_tabs, rows_per_batch, cache, tm):
    r = x.shape[0]
    nt = r // tm
    tpb = rows_per_batch // tm
    nq = GQA_Q_HEADS * HEAD_DIM
    nkv = GQA_KV_HEADS * HEAD_DIM
    row = lambda i: (i, 0)
    const = lambda i: (0, 0)
    in_specs = [pl.BlockSpec((tm, D_MODEL), row),
                pl.BlockSpec((1, 1, 6 * D_MODEL), lambda i: (i // tpb, 0, 0)),
                pl.BlockSpec((1, D_MODEL), const),
                pl.BlockSpec((D_MODEL, nq + 2 * nkv), const),
                pl.BlockSpec((1, LANES), const),
                pl.BlockSpec((1, LANES), const)]
    args = [x, mod, norm1, w, gain_q, gain_k]
    if rope_tabs is not None:
        in_specs += [pl.BlockSpec((tm, LANES), lambda i: (i % tpb, 0))] * 3
        args += list(rope_tabs)
    out_shape = [jax.ShapeDtypeStruct((r, nq), BF16), jax.ShapeDtypeStruct((r, nkv), BF16),
                 jax.ShapeDtypeStruct((r, nkv), BF16)]
    out_specs = [pl.BlockSpec((tm, nq), row), pl.BlockSpec((tm, nkv), row), pl.BlockSpec((tm, nkv), row)]
    if cache:
        out_shape += [jax.ShapeDtypeStruct((r, nkv), F32)] * 2
        out_specs += [pl.BlockSpec((tm, nkv), row)] * 2
    return pl.pallas_call(
        functools.partial(_odd_inproj_kernel, rope=rope_tabs is not None, cache=cache),
        grid=(nt,), in_specs=in_specs, out_specs=out_specs, out_shape=out_shape,
        compiler_params=_params(("arbitrary",)), name="odd_inproj_ctx" if cache else "odd_inproj_dec",
    )(*args)


_NT = (((1,), (1,)), ((), ()))


def _attend(qs, sources):
    outs = []
    for q in qs:
        ss = [lax.dot_general(q, k_ref[...], _NT, preferred_element_type=F32) for k_ref, _ in sources]
        m = functools.reduce(jnp.maximum, [jnp.max(s, axis=-1, keepdims=True) for s in ss])
        ps = [jnp.exp2(s - m) for s in ss]
        l = functools.reduce(jnp.add, [jnp.sum(p, axis=-1, keepdims=True) for p in ps])
        acc = functools.reduce(jnp.add, [jnp.dot(p.astype(BF16), v_ref[...], preferred_element_type=F32)
                                         for p, (_, v_ref) in zip(ps, sources)])
        outs.append(acc / l)
    return outs


def _da_attn_kernel(*refs, n_src, lam_init):
    q_ref, lam_ref, gn_ref = refs[:3]
    kv = refs[3:3 + 2 * n_src]
    o_ref = refs[3 + 2 * n_src]
    sources = [(kv[2 * i], kv[2 * i + 1]) for i in range(n_src)]
    q = q_ref[...]
    lo = _lane_lo()
    zero = jnp.zeros_like(q)
    a1, a2 = _attend([jnp.where(lo, q, zero), jnp.where(lo, zero, q)], sources)
    lam = lam_ref[...]
    lam_full = (jnp.exp(jnp.sum(lam[0:1] * lam[1:2], axis=-1, keepdims=True))
                - jnp.exp(jnp.sum(lam[2:3] * lam[3:4], axis=-1, keepdims=True)) + lam_init)
    o = a1 - lam_full * a2
    o_ref[...] = (_rms_rows(o, gn_ref[...]) * (1.0 - lam_init)).astype(BF16)


def _da_attention(q, k, v, ck, cv, lam, gain, batch, lam_init, tq):
    r = q.shape[0]
    s = r // batch
    nq = s // tq
    qmap = lambda b, h, i: (b * nq + i, h)
    kvmap = lambda b, h, i: (b, h)
    in_specs = [pl.BlockSpec((tq, LANES), qmap),
                pl.BlockSpec(lam.shape, lambda b, h, i: (0, 0)),
                pl.BlockSpec((1, LANES), lambda b, h, i: (0, 0)),
                pl.BlockSpec((s, LANES), kvmap), pl.BlockSpec((s, LANES), kvmap)]
    args = [q, lam, gain, k, v]
    n_src = 1
    if ck is not None:
        past = ck.shape[0] // batch
        in_specs += [pl.BlockSpec((past, LANES), kvmap)] * 2
        args += [ck, cv]
        n_src = 2
    return pl.pallas_call(
        functools.partial(_da_attn_kernel, n_src=n_src, lam_init=lam_init),
        grid=(batch, DA_HEADS, nq), in_specs=in_specs,
        out_specs=pl.BlockSpec((tq, LANES), qmap),
        out_shape=jax.ShapeDtypeStruct((r, SEG), BF16),
        compiler_params=_params(("arbitrary", "arbitrary", "arbitrary")),
        name="da_attn_dec" if ck is not None else "da_attn_ctx",
    )(*args)


def _gqa_attn_kernel(*refs, n_src):
    q_ref = refs[0]
    kv = refs[1:1 + 2 * n_src]
    o_ref = refs[1 + 2 * n_src]
    sources = [(kv[2 * i], kv[2 * i + 1]) for i in range(n_src)]
    half = pl.program_id(3) // (GQA_Q_HEADS // GQA_KV_HEADS // 2)
    lo = _lane_lo()
    keep = jnp.where(lo, 0, 1) == half
    qp = q_ref[...].astype(F32)
    qr = pltpu.roll(qp, HEAD_DIM, 1)
    qa = jnp.where(keep, jnp.where(lo, qp, qr), 0.0).astype(BF16)
    qb = jnp.where(keep, jnp.where(lo, qr, qp), 0.0).astype(BF16)
    oa, ob = _attend([qa, qb], sources)
    first = jnp.where(keep, oa, pltpu.roll(oa, HEAD_DIM, 1))
    second = jnp.where(keep, ob, pltpu.roll(ob, HEAD_DIM, 1))
    o_ref[...] = jnp.where(lo, first, second).astype(BF16)


def _gqa_attention(q, k, v, ck, cv, batch, tq):
    r = q.shape[0]
    s = r // batch
    nq = s // tq
    kv_pairs = GQA_KV_HEADS // 2
    q_pairs = GQA_Q_HEADS // 2 // kv_pairs
    qmap = lambda b, j, i, p: (b * nq + i, j * q_pairs + p)
    kvmap = lambda b, j, i, p: (b, j)
    in_specs = [pl.BlockSpec((tq, LANES), qmap), pl.BlockSpec((s, LANES), kvmap), pl.BlockSpec((s, LANES), kvmap)]
    args = [q, k, v]
    n_src = 1
    if ck is not None:
        past = ck.shape[0] // batch
        in_specs += [pl.BlockSpec((past, LANES), kvmap)] * 2
        args += [ck, cv]
        n_src = 2
    return pl.pallas_call(
        functools.partial(_gqa_attn_kernel, n_src=n_src),
        grid=(batch, kv_pairs, nq, q_pairs), in_specs=in_specs,
        out_specs=pl.BlockSpec((tq, LANES), qmap),
        out_shape=jax.ShapeDtypeStruct(q.shape, BF16),
        compiler_params=_params(("arbitrary",) * 4),
        name="gqa_attn_dec" if ck is not None else "gqa_attn_ctx",
    )(*args)


def _log_sigmoid(x):
    return jnp.minimum(x, 0.0) - jnp.log1p(jnp.exp(-jnp.abs(x)))


def _mlstm_chunk(q_ref, k_ref, v_ref, g_ref, gt_ref, c_s, n_s, m_s, h_ref, head, direction):
    L = ML_CHUNK
    q = q_ref[...]
    k = k_ref[...]
    v = v_ref[...]
    i_idx = direction * 2 * ML_HEADS + head
    f_idx = i_idx + ML_HEADS
    g = g_ref[...]
    gt = gt_ref[...]
    lane = lax.broadcasted_iota(jnp.int32, (1, N_GATES), 1)
    sub = lax.broadcasted_iota(jnp.int32, (N_GATES, 1), 0)
    i_col = jnp.sum(jnp.where(lane == i_idx, g, 0.0), axis=1, keepdims=True)
    f_col = jnp.sum(jnp.where(lane == f_idx, g, 0.0), axis=1, keepdims=True)
    i_row = jnp.sum(jnp.where(sub == i_idx, gt, 0.0), axis=0, keepdims=True)
    f_row = jnp.sum(jnp.where(sub == f_idx, gt, 0.0), axis=0, keepdims=True)
    lf_col = _log_sigmoid(f_col)
    lf_row = _log_sigmoid(f_row)
    t_i = lax.broadcasted_iota(jnp.int32, (L, L), 0)
    s_i = lax.broadcasted_iota(jnp.int32, (L, L), 1)
    vis = (s_i <= t_i) if direction == 0 else (s_i >= t_i)
    vis_t = (t_i <= s_i) if direction == 0 else (t_i >= s_i)
    b_col = jnp.sum(jnp.where(vis, lf_row, 0.0), axis=1, keepdims=True)
    b_row = jnp.sum(jnp.where(vis_t, lf_col, 0.0), axis=0, keepdims=True)
    b_last = jnp.sum(lf_row, axis=1, keepdims=True)
    m_prev = m_s[:, 0:1]
    dmat = jnp.where(vis, b_col - b_row + i_row, -jnp.inf)
    inter = b_col + m_prev
    m_t = jnp.maximum(inter, jnp.max(dmat, axis=1, keepdims=True))
    w_intra = jnp.exp(dmat - m_t)
    w_inter = jnp.exp(inter - m_t)
    qk = lax.dot_general(q, k, _NT, preferred_element_type=F32) * w_intra
    c_old = c_s[...]
    n_old = n_s[...]
    qf = q.astype(F32)
    kf = k.astype(F32)
    num = (jnp.dot(qk.astype(BF16), v, preferred_element_type=F32)
           + w_inter * lax.dot_general(q, c_old.astype(BF16), _NT, preferred_element_type=F32))
    den = jnp.sum(qk, axis=1, keepdims=True) + w_inter * jnp.sum(qf * n_old, axis=1, keepdims=True)
    h_ref[...] = num / jnp.maximum(jnp.abs(den), jnp.exp(-m_t))
    m_new = m_t[L - 1:L, :] if direction == 0 else m_t[0:1, :]
    w_s = jnp.exp(b_last - b_col + i_col - m_new)
    decay = jnp.exp(b_last + m_prev - m_new)
    vw = (v.astype(F32) * w_s).T.astype(BF16)
    c_s[...] = decay * c_old + jnp.dot(vw, k, preferred_element_type=F32)
    n_s[...] = decay * n_old + jnp.sum(kf * w_s, axis=0, keepdims=True)
    m_s[...] = jnp.broadcast_to(m_new, m_s.shape)


def _mlstm_kernel(*refs, has_init):
    it = iter(refs)
    fwd = [next(it) for _ in range(5)]
    bwd = [next(it) for _ in range(5)]
    if has_init:
        c0f, c0b, n0f, n0b, m0f, m0b = [next(it) for _ in range(6)]
    hf_o, hb_o, c_o, n_o, m_o = [next(it) for _ in range(5)]
    c_f, c_b, n_f, n_b, m_f, m_b = [next(it) for _ in range(6)]
    head = pl.program_id(1)
    chunk = pl.program_id(2)

    @pl.when(chunk == 0)
    def _():
        if has_init:
            c_f[...] = c0f[0, 0, 0]
            c_b[...] = c0b[0, 0, 0]
            n_f[...] = n0f[0, 0, 0]
            n_b[...] = n0b[0, 0, 0]
            m_f[...] = m0f[0, 0, 0]
            m_b[...] = m0b[0, 0, 0]
        else:
            for ref in (c_f, c_b, n_f, n_b, m_f, m_b):
                ref[...] = jnp.zeros(ref.shape, F32)

    _mlstm_chunk(*fwd, c_f, n_f, m_f, hf_o, head, 0)
    _mlstm_chunk(*bwd, c_b, n_b, m_b, hb_o, head, 1)

    @pl.when(chunk == pl.num_programs(2) - 1)
    def _():
        c_o[0, 0, 0] = c_f[...]
        c_o[0, 1, 0] = c_b[...]
        n_o[0, 0, 0] = n_f[...]
        n_o[0, 1, 0] = n_b[...]
        m_o[0, 0, 0] = m_f[...]
        m_o[0, 1, 0] = m_b[...]


def _mlstm(q, k, v, g, gt, init, batch):
    r = q.shape[0]
    L = ML_CHUNK
    nc = r // batch // L
    fmap = lambda b, h, c: (b * nc + c, h)
    bmap = lambda b, h, c: (b * nc + nc - 1 - c, h)
    fg = lambda b, h, c: (b * nc + c, 0)
    bg = lambda b, h, c: (b * nc + nc - 1 - c, 0)
    fgt = lambda b, h, c: (0, b * nc + c)
    bgt = lambda b, h, c: (0, b * nc + nc - 1 - c)

    def side(m, mg, mgt):
        return [pl.BlockSpec((L, ML_DIM), m)] * 3 + [pl.BlockSpec((L, N_GATES), mg), pl.BlockSpec((N_GATES, L), mgt)]

    in_specs = side(fmap, fg, fgt) + side(bmap, bg, bgt)
    args = [q, k, v, g, gt] * 2
    if init is not None:
        c0, n0, m0 = init
        for arr, shp in ((c0, (ML_DIM, ML_DIM)), (n0, (1, ML_DIM)), (m0, (1, ML_DIM))):
            for d in range(2):
                in_specs.append(pl.BlockSpec((1, 1, 1) + shp, lambda b, h, c, d=d: (b, d, h, 0, 0)))
                args.append(arr)
    state = lambda b, h, c: (b, 0, h, 0, 0)
    out_shape = [jax.ShapeDtypeStruct((r, ML_HEADS * ML_DIM), F32)] * 2 + [
        jax.ShapeDtypeStruct((batch, 2, ML_HEADS, ML_DIM, ML_DIM), F32),
        jax.ShapeDtypeStruct((batch, 2, ML_HEADS, 1, ML_DIM), F32),
        jax.ShapeDtypeStruct((batch, 2, ML_HEADS, 1, ML_DIM), F32)]
    out_specs = [pl.BlockSpec((L, ML_DIM), fmap), pl.BlockSpec((L, ML_DIM), bmap),
                 pl.BlockSpec((1, 2, 1, ML_DIM, ML_DIM), state),
                 pl.BlockSpec((1, 2, 1, 1, ML_DIM), state),
                 pl.BlockSpec((1, 2, 1, 1, ML_DIM), state)]
    scratch = [pltpu.VMEM((ML_DIM, ML_DIM), F32)] * 2 + [pltpu.VMEM((1, ML_DIM), F32)] * 4
    return pl.pallas_call(
        functools.partial(_mlstm_kernel, has_init=init is not None),
        grid=(batch, ML_HEADS, nc), in_specs=in_specs, out_specs=out_specs, out_shape=out_shape,
        scratch_shapes=scratch,
        compiler_params=_params(("arbitrary", "arbitrary", "arbitrary")),
        name="mlstm_dec" if init is not None else "mlstm_ctx",
    )(*args)


def _gate(mod_ref, which):
    return mod_ref[0, :, (3 * which + 2) * D_MODEL:(3 * which + 3) * D_MODEL]


def _even_outproj_kernel(da_ref, hf_ref, hb_ref, mo_ref, gn_ref, w_ref, x_ref, mod_ref, o_ref):
    acc = jnp.dot(da_ref[...], w_ref[0:SEG, :], preferred_element_type=F32)
    for hh in range(ML_HEADS):
        sl = slice(hh * ML_DIM, (hh + 1) * ML_DIM)
        y = _rms_rows(hf_ref[:, sl] + hb_ref[:, sl], gn_ref[...]) * mo_ref[:, sl].astype(F32)
        acc += jnp.dot(y.astype(BF16), w_ref[SEG + hh * ML_DIM:SEG + (hh + 1) * ML_DIM, :],
                       preferred_element_type=F32)
    o_ref[...] = x_ref[...] + _gate(mod_ref, 0) * acc


def _even_outproj(da, hf, hb, mo, gain, w, x, mod, rows_per_batch, tm):
    r = x.shape[0]
    tpb = rows_per_batch // tm
    row = lambda i: (i, 0)
    const = lambda i: (0, 0)
    return pl.pallas_call(
        _even_outproj_kernel, grid=(r // tm,),
        in_specs=[pl.BlockSpec((tm, SEG), row)] * 4 + [
            pl.BlockSpec((1, ML_DIM), const), pl.BlockSpec(w.shape, const),
            pl.BlockSpec((tm, D_MODEL), row), pl.BlockSpec((1, 1, 6 * D_MODEL), lambda i: (i // tpb, 0, 0))],
        out_specs=pl.BlockSpec((tm, D_MODEL), row),
        out_shape=jax.ShapeDtypeStruct(x.shape, F32),
        compiler_params=_params(("arbitrary",)), name="even_outproj",
    )(da, hf, hb, mo, gain, w, x, mod)


def _odd_outproj_kernel(a_ref, w_ref, x_ref, mod_ref, o_ref):
    acc = jnp.dot(a_ref[...], w_ref[...], preferred_element_type=F32)
    o_ref[...] = x_ref[...] + _gate(mod_ref, 0) * acc


def _odd_outproj(a, w, x, mod, rows_per_batch, tm):
    r = x.shape[0]
    tpb = rows_per_batch // tm
    row = lambda i: (i, 0)
    return pl.pallas_call(
        _odd_outproj_kernel, grid=(r // tm,),
        in_specs=[pl.BlockSpec((tm, a.shape[1]), row), pl.BlockSpec(w.shape, lambda i: (0, 0)),
                  pl.BlockSpec((tm, D_MODEL), row), pl.BlockSpec((1, 1, 6 * D_MODEL), lambda i: (i // tpb, 0, 0))],
        out_specs=pl.BlockSpec((tm, D_MODEL), row),
        out_shape=jax.ShapeDtypeStruct(x.shape, F32),
        compiler_params=_params(("arbitrary",)), name="odd_outproj",
    )(a, w, x, mod)


def _ffn_kernel(x_ref, mod_ref, n2_ref, wi_ref, wo_ref, o_ref):
    hb = _modulated(x_ref, mod_ref, n2_ref, 1).astype(BF16)
    acc = jnp.zeros(o_ref.shape, F32)
    for c in range(D_FF // FF_CHUNK):
        sl = slice(c * FF_CHUNK, (c + 1) * FF_CHUNK)
        gate = jnp.dot(hb, wi_ref[:, sl], preferred_element_type=F32)
        up = jnp.dot(hb, wi_ref[:, D_FF + c * FF_CHUNK:D_FF + (c + 1) * FF_CHUNK], preferred_element_type=F32)
        act = (gate * _sigmoid(gate) * up).astype(BF16)
        acc += jnp.dot(act, wo_ref[sl, :], preferred_element_type=F32)
    o_ref[...] = x_ref[...] + _gate(mod_ref, 1) * acc


def _ffn(x, mod, norm2, w_in, w_out, rows_per_batch, tm):
    r = x.shape[0]
    tpb = rows_per_batch // tm
    row = lambda i: (i, 0)
    const = lambda i: (0, 0)
    return pl.pallas_call(
        _ffn_kernel, grid=(r // tm,),
        in_specs=[pl.BlockSpec((tm, D_MODEL), row), pl.BlockSpec((1, 1, 6 * D_MODEL), lambda i: (i // tpb, 0, 0)),
                  pl.BlockSpec((1, D_MODEL), const), pl.BlockSpec(w_in.shape, const), pl.BlockSpec(w_out.shape, const)],
        out_specs=pl.BlockSpec((tm, D_MODEL), row),
        out_shape=jax.ShapeDtypeStruct(x.shape, F32),
        compiler_params=_params(("arbitrary",)), name="ffn",
    )(x, mod, norm2, w_in, w_out)


def _rope_tables(n_tokens):
    t = jnp.arange(n_tokens)
    pos = jnp.stack([(t // GRID_W).astype(F32), (t % GRID_W).astype(F32)], axis=1)
    freqs = ROPE_THETA ** (-jnp.arange(ROPE_PAIRS, dtype=F32) / ROPE_PAIRS)
    ang = pos[:, :, None] * freqs
    cos, sin = jnp.cos(ang), jnp.sin(ang)
    zero = jnp.zeros_like(sin)
    lay = lambda first, second: jnp.tile(jnp.stack([first, second], axis=2).reshape(n_tokens, HEAD_DIM), (1, 2))
    return lay(cos, cos), lay(-sin, zero), lay(zero, sin)


def _trunk(x, mods, li, P, batch, rope_tabs, ctx, tm):
    lj = li // 2
    rows_per_batch = x.shape[0] // batch
    mod_rpb = rows_per_batch if mods.shape[0] > 1 else x.shape[0]
    is_ctx = ctx is None
    new = None
    if li % 2 == 0:
        lam_init = 0.8 - 0.6 * math.exp(-0.3 * li)
        outs = _even_inproj(x, mods, P['norm1'][li], P['w_even_main'][lj], P['w_even_gate'][lj], P['b_gate'][lj],
                            P['gq_even'][lj], P['gk_even'][lj], rope_tabs, mod_rpb, is_ctx, tm)
        q, k, v, mq, mk, mv, mo, g, gt = outs[:9]
        ck, cv, init = (None, None, None) if is_ctx else ctx
        da = _da_attention(q, k, v, ck, cv, P['lam_even'][lj], P['da_norm'][lj], batch, lam_init, ATTN_TQ)
        hf, hb, c_n, n_n, m_n = _mlstm(mq, mk, mv, g, gt, init, batch)
        x = _even_outproj(da, hf, hb, mo, P['ml_norm'][lj], P['w_out_even'][lj], x, mods, mod_rpb, tm)
        if is_ctx:
            new = (outs[9], outs[10], c_n, n_n, m_n)
    else:
        outs = _odd_inproj(x, mods, P['norm1'][li], P['w_in_odd'][lj], P['gq_odd'][lj], P['gk_odd'][lj],
                           rope_tabs, mod_rpb, is_ctx, tm)
        q, k, v = outs[:3]
        ck, cv = (None, None) if is_ctx else ctx
        a = _gqa_attention(q, k, v, ck, cv, batch, ATTN_TQ)
        x = _odd_outproj(a, P['w_out_odd'][lj], x, mods, mod_rpb, tm)
        if is_ctx:
            new = (outs[3], outs[4])
    x = _ffn(x, mods, P['norm2'][li], P['w_ffn_in'][li], P['w_ffn_out'][li], mod_rpb, tm)
    return x, new


def kernel(x_prompt, x_sample, c, cache_da_k, cache_da_v, state_mlstm_C, state_mlstm_n, state_mlstm_m, cache_gqa_k, cache_gqa_v, c_ctx, norm1, norm2, w_mod, b_mod, w_in_even, b_gate_even, qk_gain_even, lam_even, da_norm_even, ml_norm_even, w_out_even, w_in_odd, qk_gain_odd, w_out_odd, w_ffn_in, w_ffn_out):
    batch, seq, _ = x_prompt.shape
    dbatch, dseq, _ = x_sample.shape
    depth = norm1.shape[0]
    n_even, n_odd = w_in_even.shape[0], w_in_odd.shape[0]
    past = cache_da_k.shape[2]
    tm = 512

    P = dict(
        norm1=norm1.reshape(depth, 1, D_MODEL), norm2=norm2.reshape(depth, 1, D_MODEL),
        w_even_main=w_in_even[:, :, :EVEN_MAIN].astype(BF16),
        w_even_gate=jnp.pad(w_in_even[:, :, EVEN_MAIN:], ((0, 0), (0, 0), (0, LANES - N_GATES))),
        b_gate=jnp.pad(b_gate_even.reshape(n_even, 1, N_GATES), ((0, 0), (0, 0), (0, LANES - N_GATES))),
        gq_even=qk_gain_even[:, 0].reshape(n_even, 1, LANES), gk_even=qk_gain_even[:, 1].reshape(n_even, 1, LANES),
        lam_even=lam_even, da_norm=da_norm_even.reshape(n_even, 1, LANES),
        ml_norm=ml_norm_even.reshape(n_even, 1, ML_DIM),
        w_out_even=w_out_even.astype(BF16), w_in_odd=w_in_odd.astype(BF16),
        gq_odd=jnp.tile(qk_gain_odd[:, 0], (1, 2)).reshape(n_odd, 1, LANES),
        gk_odd=jnp.tile(qk_gain_odd[:, 1], (1, 2)).reshape(n_odd, 1, LANES),
        w_out_odd=w_out_odd.astype(BF16), w_ffn_in=w_ffn_in.astype(BF16), w_ffn_out=w_ffn_out.astype(BF16))

    cond8 = jnp.concatenate([c_ctx[None, :], c, jnp.zeros((8 - 1 - dbatch, D_MODEL), F32)], axis=0)
    mod = _modulation(cond8, w_mod, b_mod)
    rope_tabs = _rope_tables(dseq)

    x = x_prompt.reshape(batch * seq, D_MODEL)
    ctx_even, ctx_odd = [], []
    for li in range(depth):
        x, new = _trunk(x, mod[li, 0:1].reshape(1, 1, 6 * D_MODEL), li, P, batch, None, None, tm)
        (ctx_even if li % 2 == 0 else ctx_odd).append(new)
    y_prompt = x.reshape(batch, seq, D_MODEL)

    x = x_sample.reshape(dbatch * dseq, D_MODEL)
    for li in range(depth):
        j = li // 2
        if li % 2 == 0:
            ctx = (cache_da_k[:, j].reshape(dbatch * past, SEG).astype(BF16),
                   cache_da_v[:, j].reshape(dbatch * past, SEG).astype(BF16),
                   (state_mlstm_C[:, j], state_mlstm_n[:, j][:, :, :, None, :],
                    jnp.broadcast_to(state_mlstm_m[:, j][:, :, :, None, None], (dbatch, 2, ML_HEADS, 1, ML_DIM))))
        else:
            ctx = (cache_gqa_k[:, j].reshape(dbatch * past, GQA_KV_HEADS * HEAD_DIM).astype(BF16),
                   cache_gqa_v[:, j].reshape(dbatch * past, GQA_KV_HEADS * HEAD_DIM).astype(BF16))
        x, _ = _trunk(x, mod[li, 1:1 + dbatch].reshape(dbatch, 1, 6 * D_MODEL), li, P, dbatch, rope_tabs, ctx, tm)
    y_sample = x.reshape(dbatch, dseq, D_MODEL)

    new_da_k = jnp.stack([t[0].reshape(batch, seq, DA_HEADS, 2, HEAD_DIM) for t in ctx_even], axis=1)
    new_da_v = jnp.stack([t[1].reshape(batch, seq, DA_HEADS, 2 * HEAD_DIM) for t in ctx_even], axis=1)
    new_c = jnp.stack([t[2] for t in ctx_even], axis=1)
    new_n = jnp.stack([t[3][:, :, :, 0, :] for t in ctx_even], axis=1)
    new_m = jnp.stack([t[4][:, :, :, 0, 0] for t in ctx_even], axis=1)
    new_gqa_k = jnp.stack([t[0].reshape(batch, seq, GQA_KV_HEADS, HEAD_DIM) for t in ctx_odd], axis=1)
    new_gqa_v = jnp.stack([t[1].reshape(batch, seq, GQA_KV_HEADS, HEAD_DIM) for t in ctx_odd], axis=1)
    return (y_prompt, y_sample, new_da_k, new_da_v, new_c, new_n, new_m, new_gqa_k, new_gqa_v)
```

```python
import functools
import math

import jax
import jax.numpy as jnp
from jax import lax
from jax.experimental import pallas as pl
from jax.experimental.pallas import tpu as pltpu

F32 = jnp.float32
BF16 = jnp.bfloat16
HIGHEST = lax.Precision.HIGHEST

D_MODEL = 1024
GRID_W = 64
HEAD_DIM = 64
ROPE_PAIRS = HEAD_DIM // 4
ROPE_THETA = 10000.0
EPS = 1e-6
DA_HEADS = 4
ML_HEADS = 4
ML_DIM = 128
N_GATES = 2 * 2 * ML_HEADS
GQA_Q_HEADS = 16
GQA_KV_HEADS = 4
SEG = 512
EVEN_MAIN = 7 * SEG
D_FF = -(-8 * D_MODEL // (3 * 256)) * 256

LANES = 128
VMEM_LIMIT = 56 * 2 ** 20

ATTN_TQ = 256
Q_SCALE = math.log2(math.e) * HEAD_DIM ** -0.5
ML_CHUNK = 128
FF_CHUNK = 256


def _params(sem, vmem=VMEM_LIMIT):
    return pltpu.CompilerParams(dimension_semantics=sem, vmem_limit_bytes=vmem)


def _rms_rows(x, g):
    return x * lax.rsqrt(jnp.mean(x * x, axis=-1, keepdims=True) + EPS) * g


def _lane_lo():
    return lax.broadcasted_iota(jnp.int32, (1, LANES), 1) < HEAD_DIM


def _half_rms(x, gain, lo):
    x2 = x * x
    s_lo = jnp.sum(jnp.where(lo, x2, 0.0), axis=-1, keepdims=True)
    s_hi = jnp.sum(jnp.where(lo, 0.0, x2), axis=-1, keepdims=True)
    r = jnp.where(lo, lax.rsqrt(s_lo * (1.0 / HEAD_DIM) + EPS), lax.rsqrt(s_hi * (1.0 / HEAD_DIM) + EPS))
    return x * r * gain


def _rope(y, cos, sin_a, sin_b):
    return y * cos + pltpu.roll(y, LANES - ROPE_PAIRS, 1) * sin_a + pltpu.roll(y, ROPE_PAIRS, 1) * sin_b


def _sigmoid(x):
    return 1.0 / (1.0 + jnp.exp(-x))


def _store_with_ones(v_o, v):
    ones = jnp.ones((v.shape[0], LANES), BF16)
    for g in range(v.shape[1] // LANES):
        v_o[:, 2 * g * LANES:(2 * g + 1) * LANES] = v[:, g * LANES:(g + 1) * LANES].astype(BF16)
        v_o[:, (2 * g + 1) * LANES:(2 * g + 2) * LANES] = ones


def _mod_kernel(cond_ref, w_ref, b_ref, o_ref):
    c = cond_ref[...]
    a = c * _sigmoid(c)
    o_ref[0] = jnp.dot(a, w_ref[0], preferred_element_type=F32, precision=HIGHEST) + b_ref[0]


def _modulation(cond8, w_mod, b_mod):
    depth, _, n = w_mod.shape
    tn = 1024
    return pl.pallas_call(
        _mod_kernel,
        grid=(depth, n // tn),
        in_specs=[pl.BlockSpec((8, D_MODEL), lambda l, j: (0, 0)),
                  pl.BlockSpec((1, D_MODEL, tn), lambda l, j: (l, 0, j)),
                  pl.BlockSpec((1, 1, tn), lambda l, j: (l, 0, j))],
        out_specs=pl.BlockSpec((1, 8, tn), lambda l, j: (l, 0, j)),
        out_shape=jax.ShapeDtypeStruct((depth, 8, n), F32),
        compiler_params=_params(("arbitrary", "arbitrary")), name="modulation",
    )(cond8, w_mod, b_mod.reshape(depth, 1, n))


def _modulated(x_ref, mod_ref, norm_ref, which):
    sh = mod_ref[0, :, (3 * which) * D_MODEL:(3 * which + 1) * D_MODEL]
    sc = mod_ref[0, :, (3 * which + 1) * D_MODEL:(3 * which + 2) * D_MODEL]
    return _rms_rows(x_ref[...], norm_ref[...]) * (1.0 + sc) + sh


def _even_inproj_kernel(*refs, rope, cache):
    it = iter(refs)
    x_ref, mod_ref, n1_ref, w_ref, bg_ref, gq_ref, gk_ref = [next(it) for _ in range(7)]
    if rope:
        cos_ref, sa_ref, sb_ref = [next(it) for _ in range(3)]
    q_o, k_o, v_o, mq_o, mk_o, mv_o, mo_o, g_o, gt_o = [next(it) for _ in range(9)]
    if cache:
        kf_o, vf_o = [next(it) for _ in range(2)]

    hb = _modulated(x_ref, mod_ref, n1_ref, 0).astype(BF16)
    lo = _lane_lo()

    def seg(i):
        return jnp.dot(hb, w_ref[:, i * SEG:(i + 1) * SEG], preferred_element_type=F32)

    def qk_norm(p, gain_ref, hh):
        y = _half_rms(p[:, hh * LANES:(hh + 1) * LANES], gain_ref[...], lo)
        if rope:
            y = _rope(y, cos_ref[...], sa_ref[...], sb_ref[...])
        return y

    p = seg(0)
    for hh in range(DA_HEADS):
        q_o[:, hh * LANES:(hh + 1) * LANES] = (qk_norm(p, gq_ref, hh) * Q_SCALE).astype(BF16)
    p = seg(1)
    for hh in range(DA_HEADS):
        sl = slice(hh * LANES, (hh + 1) * LANES)
        if cache:
            y = _half_rms(p[:, sl], gk_ref[...], lo)
            kf_o[:, sl] = y
        else:
            y = qk_norm(p, gk_ref, hh)
        k_o[:, sl] = y.astype(BF16)
    p = seg(2)
    if cache:
        v_o[...] = p.astype(BF16)
        vf_o[...] = p
    else:
        _store_with_ones(v_o, p)
    mq_o[...] = seg(3).astype(BF16)
    mk_o[...] = (seg(4) * (ML_DIM ** -0.5)).astype(BF16)
    mv_o[...] = seg(5).astype(BF16)
    mo_o[...] = _sigmoid(seg(6)).astype(BF16)
    pg = jnp.dot(hb, w_ref[:, EVEN_MAIN:EVEN_MAIN + LANES], preferred_element_type=F32) + bg_ref[...]
    g_o[...] = pg[:, :N_GATES]
    gt_o[...] = pg.T[:N_GATES, :]


def _even_inproj(x, mod, norm1, w_main, b_gate, gain_q, gain_k, rope_tabs, rows_per_batch, cache, tm):
    r = x.shape[0]
    nt = r // tm
    tpb = rows_per_batch // tm
    row = lambda i: (i, 0)
    const = lambda i: (0, 0)
    in_specs = [pl.BlockSpec((tm, D_MODEL), row),
                pl.BlockSpec((1, 1, 6 * D_MODEL), lambda i: (i // tpb, 0, 0)),
                pl.BlockSpec((1, D_MODEL), const),
                pl.BlockSpec((D_MODEL, EVEN_MAIN + LANES), const),
                pl.BlockSpec((1, LANES), const),
                pl.BlockSpec((1, LANES), const),
                pl.BlockSpec((1, LANES), const)]
    args = [x, mod, norm1, w_main, b_gate, gain_q, gain_k]
    if rope_tabs is not None:
        in_specs += [pl.BlockSpec((tm, LANES), lambda i: (i % tpb, 0))] * 3
        args += list(rope_tabs)
    out_shape = [jax.ShapeDtypeStruct((r, SEG), BF16)] * 7
    out_specs = [pl.BlockSpec((tm, SEG), row)] * 7
    if not cache:
        out_shape[2] = jax.ShapeDtypeStruct((r, 2 * SEG), BF16)
        out_specs[2] = pl.BlockSpec((tm, 2 * SEG), row)
    out_shape += [jax.ShapeDtypeStruct((r, N_GATES), F32), jax.ShapeDtypeStruct((N_GATES, r), F32)]
    out_specs += [pl.BlockSpec((tm, N_GATES), row), pl.BlockSpec((N_GATES, tm), lambda i: (0, i))]
    if cache:
        out_shape += [jax.ShapeDtypeStruct((r, SEG), F32)] * 2
        out_specs += [pl.BlockSpec((tm, SEG), row)] * 2
    return pl.pallas_call(
        functools.partial(_even_inproj_kernel, rope=rope_tabs is not None, cache=cache),
        grid=(nt,), in_specs=in_specs, out_specs=out_specs, out_shape=out_shape,
        compiler_params=_params(("arbitrary",)), name="even_inproj_ctx" if cache else "even_inproj_dec",
    )(*args)


def _odd_inproj_kernel(*refs, rope, cache):
    it = iter(refs)
    x_ref, mod_ref, n1_ref, w_ref, gq_ref, gk_ref = [next(it) for _ in range(6)]
    if rope:
        cos_ref, sa_ref, sb_ref = [next(it) for _ in range(3)]
    q_o, k_o, v_o = [next(it) for _ in range(3)]
    if cache:
        kf_o, vf_o = [next(it) for _ in range(2)]

    hb = _modulated(x_ref, mod_ref, n1_ref, 0).astype(BF16)
    lo = _lane_lo()
    nq = GQA_Q_HEADS * HEAD_DIM
    nkv = GQA_KV_HEADS * HEAD_DIM

    def qk_norm(t, gain_ref):
        y = _half_rms(t, gain_ref[...], lo)
        if rope:
            y = _rope(y, cos_ref[...], sa_ref[...], sb_ref[...])
        return y

    for c in range(nq // SEG):
        p = jnp.dot(hb, w_ref[:, c * SEG:(c + 1) * SEG], preferred_element_type=F32)
        for hh in range(SEG // LANES):
            sl = slice(c * SEG + hh * LANES, c * SEG + (hh + 1) * LANES)
            q_o[:, sl] = (qk_norm(p[:, hh * LANES:(hh + 1) * LANES], gq_ref) * Q_SCALE).astype(BF16)
    p = jnp.dot(hb, w_ref[:, nq:nq + 2 * nkv], preferred_element_type=F32)
    for hh in range(nkv // LANES):
        sl = slice(hh * LANES, (hh + 1) * LANES)
        if cache:
            y = _half_rms(p[:, sl], gk_ref[...], lo)
            kf_o[:, sl] = y
        else:
            y = qk_norm(p[:, sl], gk_ref)
        k_o[:, sl] = y.astype(BF16)
    v = p[:, nkv:2 * nkv]
    if cache:
        v_o[...] = v.astype(BF16)
        vf_o[...] = v
    else:
        _store_with_ones(v_o, v)


def _odd_inproj(x, mod, norm1, w, gain_q, gain_k, rope_tabs, rows_per_batch, cache, tm):
    r = x.shape[0]
    nt = r // tm
    tpb = rows_per_batch // tm
    nq = GQA_Q_HEADS * HEAD_DIM
    nkv = GQA_KV_HEADS * HEAD_DIM
    row = lambda i: (i, 0)
    const = lambda i: (0, 0)
    in_specs = [pl.BlockSpec((tm, D_MODEL), row),
                pl.BlockSpec((1, 1, 6 * D_MODEL), lambda i: (i // tpb, 0, 0)),
                pl.BlockSpec((1, D_MODEL), const),
                pl.BlockSpec((D_MODEL, nq + 2 * nkv), const),
                pl.BlockSpec((1, LANES), const),
                pl.BlockSpec((1, LANES), const)]
    args = [x, mod, norm1, w, gain_q, gain_k]
    if rope_tabs is not None:
        in_specs += [pl.BlockSpec((tm, LANES), lambda i: (i % tpb, 0))] * 3
        args += list(rope_tabs)
    out_shape = [jax.ShapeDtypeStruct((r, nq), BF16), jax.ShapeDtypeStruct((r, nkv), BF16),
                 jax.ShapeDtypeStruct((r, nkv), BF16)]
    out_specs = [pl.BlockSpec((tm, nq), row), pl.BlockSpec((tm, nkv), row), pl.BlockSpec((tm, nkv), row)]
    if not cache:
        out_shape[2] = jax.ShapeDtypeStruct((r, 2 * nkv), BF16)
        out_specs[2] = pl.BlockSpec((tm, 2 * nkv), row)
    if cache:
        out_shape += [jax.ShapeDtypeStruct((r, nkv), F32)] * 2
        out_specs += [pl.BlockSpec((tm, nkv), row)] * 2
    return pl.pallas_call(
        functools.partial(_odd_inproj_kernel, rope=rope_tabs is not None, cache=cache),
        grid=(nt,), in_specs=in_specs, out_specs=out_specs, out_shape=out_shape,
        compiler_params=_params(("arbitrary",)), name="odd_inproj_ctx" if cache else "odd_inproj_dec",
    )(*args)


_NT = (((1,), (1,)), ((), ()))


def _attend(qs, sources):
    outs = []
    for q in qs:
        ss = [lax.dot_general(q, k_ref[...], _NT, preferred_element_type=F32) for k_ref, _ in sources]
        m = functools.reduce(jnp.maximum, [jnp.max(s, axis=-1, keepdims=True) for s in ss])
        ps = [jnp.exp2(s - m) for s in ss]
        l = functools.reduce(jnp.add, [jnp.sum(p, axis=-1, keepdims=True) for p in ps])
        acc = functools.reduce(jnp.add, [jnp.dot(p.astype(BF16), v_ref[...], preferred_element_type=F32)
                                         for p, (_, v_ref) in zip(ps, sources)])
        outs.append(acc / l)
    return outs


def _lam_full(lam_ref, lam_init):
    lam = lam_ref[...]
    return (jnp.exp(jnp.sum(lam[0:1] * lam[1:2], axis=-1, keepdims=True))
            - jnp.exp(jnp.sum(lam[2:3] * lam[3:4], axis=-1, keepdims=True)) + lam_init)


def _da_ctx_kernel(q_ref, k_ref, v_ref, lam_ref, gn_ref, o_ref, *, lam_init):
    lo = _lane_lo()
    lam_full = _lam_full(lam_ref, lam_init)
    for h in range(DA_HEADS):
        sl = slice(h * LANES, (h + 1) * LANES)
        q = q_ref[:, sl]
        zero = jnp.zeros_like(q)
        a1, a2 = _attend([jnp.where(lo, q, zero), jnp.where(lo, zero, q)], [(k_ref.at[:, sl], v_ref.at[:, sl])])
        o = a1 - lam_full * a2
        o_ref[:, sl] = (_rms_rows(o, gn_ref[...]) * (1.0 - lam_init)).astype(BF16)


def _da_ctx_attention(q, k, v, lam, gain, batch, lam_init):
    r = q.shape[0]
    s = r // batch
    blk = pl.BlockSpec((s, SEG), lambda b: (b, 0))
    const = lambda b: (0, 0)
    return pl.pallas_call(
        functools.partial(_da_ctx_kernel, lam_init=lam_init), grid=(batch,),
        in_specs=[blk, blk, blk, pl.BlockSpec(lam.shape, const), pl.BlockSpec((1, LANES), const)],
        out_specs=blk, out_shape=jax.ShapeDtypeStruct((r, SEG), BF16),
        compiler_params=_params(("arbitrary",)), name="da_attn_ctx",
    )(q, k, v, lam, gain)


def _gqa_ctx_kernel(q_ref, k_ref, v_ref, o_ref):
    lo = _lane_lo()
    group_pairs = GQA_Q_HEADS // GQA_KV_HEADS // 2
    for j in range(GQA_KV_HEADS // 2):
        kv = (k_ref.at[:, j * LANES:(j + 1) * LANES], v_ref.at[:, j * LANES:(j + 1) * LANES])
        for half in range(2):
            keep = lo if half == 0 else jnp.logical_not(lo)
            for pr in range(group_pairs):
                col = ((j * 2 + half) * group_pairs + pr) * LANES
                qp = q_ref[:, col:col + LANES].astype(F32)
                qr = pltpu.roll(qp, HEAD_DIM, 1)
                qa = jnp.where(keep, jnp.where(lo, qp, qr), 0.0).astype(BF16)
                qb = jnp.where(keep, jnp.where(lo, qr, qp), 0.0).astype(BF16)
                oa, ob = _attend([qa, qb], [kv])
                first = jnp.where(keep, oa, pltpu.roll(oa, HEAD_DIM, 1))
                second = jnp.where(keep, ob, pltpu.roll(ob, HEAD_DIM, 1))
                o_ref[:, col:col + LANES] = jnp.where(lo, first, second).astype(BF16)


def _gqa_ctx_attention(q, k, v, batch):
    r = q.shape[0]
    s = r // batch
    row = lambda b: (b, 0)
    return pl.pallas_call(
        _gqa_ctx_kernel, grid=(batch,),
        in_specs=[pl.BlockSpec((s, q.shape[1]), row), pl.BlockSpec((s, k.shape[1]), row),
                  pl.BlockSpec((s, v.shape[1]), row)],
        out_specs=pl.BlockSpec((s, q.shape[1]), row), out_shape=jax.ShapeDtypeStruct(q.shape, BF16),
        compiler_params=_params(("arbitrary",)), name="gqa_attn_ctx",
    )(q, k, v)


def _score_item(qs, k_ref, ck_ref, s_w, m_w):
    nk, past = k_ref.shape[0], ck_ref.shape[0]
    for a, q in enumerate(qs):
        s1 = lax.dot_general(q, k_ref[...], _NT, preferred_element_type=F32)
        s2 = lax.dot_general(q, ck_ref[...], _NT, preferred_element_type=F32)
        s_w[a, :, 0:nk] = s1
        s_w[a, :, nk:nk + past] = s2
        m_w[a] = jnp.maximum(jnp.max(s1, axis=-1, keepdims=True), jnp.max(s2, axis=-1, keepdims=True))


def _finish_item(v_ref, cv_ref, s_r, m_r):
    nk, past = v_ref.shape[0], cv_ref.shape[0]
    outs = []
    for a in range(2):
        m = m_r[a]
        p1 = jnp.exp2(s_r[a, :, 0:nk] - m).astype(BF16)
        p2 = jnp.exp2(s_r[a, :, nk:nk + past] - m).astype(BF16)
        acc = (jnp.dot(p1, v_ref[...], preferred_element_type=F32)
               + jnp.dot(p2, cv_ref[...], preferred_element_type=F32))
        outs.append(acc[:, 0:LANES] / acc[:, LANES:LANES + 1])
    return outs


def _by_parity(t, body, sa, ma, sb, mb):
    @pl.when(t == 0)
    def _():
        sb[...] = jnp.zeros(sb.shape, F32)
        mb[...] = jnp.zeros(mb.shape, F32)

    @pl.when(t % 2 == 0)
    def _():
        body(sa, ma, sb, mb)

    @pl.when(t % 2 == 1)
    def _():
        body(sb, mb, sa, ma)


def _da_pipe_kernel(q_ref, k_ref, ck_ref, v_ref, cv_ref, lam_ref, gn_ref, o_ref, sa, ma, sb, mb, *, lam_init):
    lo = _lane_lo()

    def body(s_w, m_w, s_r, m_r):
        q = q_ref[...]
        zero = jnp.zeros_like(q)
        _score_item([jnp.where(lo, q, zero), jnp.where(lo, zero, q)], k_ref, ck_ref, s_w, m_w)
        a1, a2 = _finish_item(v_ref, cv_ref, s_r, m_r)
        o = a1 - _lam_full(lam_ref, lam_init) * a2
        o_ref[...] = (_rms_rows(o, gn_ref[...]) * (1.0 - lam_init)).astype(BF16)

    _by_parity(pl.program_id(0), body, sa, ma, sb, mb)


def _gqa_pipe_kernel(q_ref, k_ref, ck_ref, v_ref, cv_ref, o_ref, sa, ma, sb, mb, *, n_items):
    t = pl.program_id(0)
    lo = _lane_lo()
    q_pairs = GQA_Q_HEADS // GQA_KV_HEADS

    def keep_of(item):
        half = (item % q_pairs) // (q_pairs // 2)
        return jnp.where(lo, 0, 1) == half

    def body(s_w, m_w, s_r, m_r):
        keep = keep_of(jnp.minimum(t, n_items - 1))
        qp = q_ref[...].astype(F32)
        qr = pltpu.roll(qp, HEAD_DIM, 1)
        qa = jnp.where(keep, jnp.where(lo, qp, qr), 0.0).astype(BF16)
        qb = jnp.where(keep, jnp.where(lo, qr, qp), 0.0).astype(BF16)
        _score_item([qa, qb], k_ref, ck_ref, s_w, m_w)
        keep = keep_of(jnp.maximum(t - 1, 0))
        oa, ob = _finish_item(v_ref, cv_ref, s_r, m_r)
        first = jnp.where(keep, oa, pltpu.roll(oa, HEAD_DIM, 1))
        second = jnp.where(keep, ob, pltpu.roll(ob, HEAD_DIM, 1))
        o_ref[...] = jnp.where(lo, first, second).astype(BF16)

    _by_parity(t, body, sa, ma, sb, mb)


def _pipe_attention(kind, q, k, v1, ck, cv1, batch, tq, extra=(), lam_init=None):
    r = q.shape[0]
    s = r // batch
    past = ck.shape[0] // batch
    nq = s // tq
    groups = k.shape[1] // LANES
    per_q = q.shape[1] // LANES // groups
    per_g = nq * per_q
    n_items = batch * groups * per_g

    def split(item):
        b, g, rest = item // (groups * per_g), (item // per_g) % groups, item % per_g
        return b, g, rest // per_q, rest % per_q

    def q_index(item):
        b, g, i, p = split(item)
        return (b * nq + i, g * per_q + p)

    def kv_index(item):
        b, g, _, _ = split(item)
        return (b, g)

    cur = lambda t: jnp.minimum(t, n_items - 1)
    prev = lambda t: jnp.maximum(t - 1, 0)
    in_specs = [pl.BlockSpec((tq, LANES), lambda t: q_index(cur(t))),
                pl.BlockSpec((s, LANES), lambda t: kv_index(cur(t))),
                pl.BlockSpec((past, LANES), lambda t: kv_index(cur(t))),
                pl.BlockSpec((s, 2 * LANES), lambda t: kv_index(prev(t))),
                pl.BlockSpec((past, 2 * LANES), lambda t: kv_index(prev(t)))]
    in_specs += [pl.BlockSpec(e.shape, lambda t: (0, 0)) for e in extra]
    if kind == "da":
        body = functools.partial(_da_pipe_kernel, lam_init=lam_init)
    else:
        body = functools.partial(_gqa_pipe_kernel, n_items=n_items)
    scratch = [pltpu.VMEM((2, tq, s + past), F32), pltpu.VMEM((2, tq, 1), F32)] * 2
    return pl.pallas_call(
        body, grid=(n_items + 1,), in_specs=in_specs,
        out_specs=pl.BlockSpec((tq, LANES), lambda t: q_index(prev(t))),
        out_shape=jax.ShapeDtypeStruct(q.shape, BF16),
        scratch_shapes=scratch,
        compiler_params=_params(("arbitrary",)),
        name=kind + "_attn_dec",
    )(q, k, ck, v1, cv1, *extra)


def _log_sigmoid(x):
    return jnp.minimum(x, 0.0) - jnp.log1p(jnp.exp(-jnp.abs(x)))


def _mlstm_chunk(q, k, v, g, gt, c_s, n_s, m_s, h_ref, cols, head, direction):
    L = ML_CHUNK
    i_idx = direction * 2 * ML_HEADS + head
    f_idx = i_idx + ML_HEADS
    i_col, f_col = g[:, i_idx:i_idx + 1], g[:, f_idx:f_idx + 1]
    i_row, f_row = gt[i_idx:i_idx + 1, :], gt[f_idx:f_idx + 1, :]
    lf_col = _log_sigmoid(f_col)
    lf_row = _log_sigmoid(f_row)
    t_i = lax.broadcasted_iota(jnp.int32, (L, L), 0)
    s_i = lax.broadcasted_iota(jnp.int32, (L, L), 1)
    vis = (s_i <= t_i) if direction == 0 else (s_i >= t_i)
    vis_t = (t_i <= s_i) if direction == 0 else (t_i >= s_i)
    b_col = jnp.sum(jnp.where(vis, lf_row, 0.0), axis=1, keepdims=True)
    b_row = jnp.sum(jnp.where(vis_t, lf_col, 0.0), axis=0, keepdims=True)
    b_last = jnp.sum(lf_row, axis=1, keepdims=True)
    m_prev = m_s[:, 0:1]
    dmat = jnp.where(vis, b_col - b_row + i_row, -jnp.inf)
    inter = b_col + m_prev
    m_t = jnp.maximum(inter, jnp.max(dmat, axis=1, keepdims=True))
    w_intra = jnp.exp(dmat - m_t)
    w_inter = jnp.exp(inter - m_t)
    qk = lax.dot_general(q, k, _NT, preferred_element_type=F32) * w_intra
    c_old = c_s[...]
    n_old = n_s[...]
    qf = q.astype(F32)
    kf = k.astype(F32)
    num = (jnp.dot(qk.astype(BF16), v, preferred_element_type=F32)
           + w_inter * lax.dot_general(q, c_old.astype(BF16), _NT, preferred_element_type=F32))
    den = jnp.sum(qk, axis=1, keepdims=True) + w_inter * jnp.sum(qf * n_old, axis=1, keepdims=True)
    h_ref[:, cols] = num / jnp.maximum(jnp.abs(den), jnp.exp(-m_t))
    m_new = m_t[L - 1:L, :] if direction == 0 else m_t[0:1, :]
    w_s = jnp.exp(b_last - b_col + i_col - m_new)
    decay = jnp.exp(b_last + m_prev - m_new)
    vw = (v.astype(F32) * w_s).T.astype(BF16)
    c_s[...] = decay * c_old + jnp.dot(vw, k, preferred_element_type=F32)
    n_s[...] = decay * n_old + jnp.sum(kf * w_s, axis=0, keepdims=True)
    m_s[...] = jnp.broadcast_to(m_new, m_s.shape)


def _mlstm_kernel(*refs, has_init):
    it = iter(refs)
    sides = [[next(it) for _ in range(5)] for _ in range(2)]
    if has_init:
        c0, n0, m0 = [next(it) for _ in range(3)]
    h_os = [next(it) for _ in range(2)]
    c_o, n_o, m_o = [next(it) for _ in range(3)]
    c_s, n_s, m_s = [next(it) for _ in range(3)]
    chunk = pl.program_id(1)

    @pl.when(chunk == 0)
    def _():
        if has_init:
            c_s[...] = c0[0]
            n_s[...] = n0[0]
            m_s[...] = m0[0]
        else:
            for ref in (c_s, n_s, m_s):
                ref[...] = jnp.zeros(ref.shape, F32)

    for d, ((q_ref, k_ref, v_ref, g_ref, gt_ref), h_o) in enumerate(zip(sides, h_os)):
        g, gt = g_ref[...], gt_ref[...]
        for h in range(ML_HEADS):
            cols = slice(h * ML_DIM, (h + 1) * ML_DIM)
            _mlstm_chunk(q_ref[:, cols], k_ref[:, cols], v_ref[:, cols], g, gt,
                         c_s.at[d, h], n_s.at[d, h], m_s.at[d, h], h_o, cols, h, d)

    @pl.when(chunk == pl.num_programs(1) - 1)
    def _():
        c_o[0] = c_s[...]
        n_o[0] = n_s[...]
        m_o[0] = m_s[...]


def _mlstm(q, k, v, g, gt, init, batch):
    r = q.shape[0]
    L = ML_CHUNK
    nc = r // batch // L
    width = ML_HEADS * ML_DIM
    fwd = lambda b, c: b * nc + c
    bwd = lambda b, c: b * nc + nc - 1 - c

    def side(pos):
        return ([pl.BlockSpec((L, width), lambda b, c: (pos(b, c), 0))] * 3
                + [pl.BlockSpec((L, N_GATES), lambda b, c: (pos(b, c), 0)),
                   pl.BlockSpec((N_GATES, L), lambda b, c: (0, pos(b, c)))])

    in_specs = side(fwd) + side(bwd)
    args = [q, k, v, g, gt] * 2
    state_shapes = [(2, ML_HEADS, ML_DIM, ML_DIM), (2, ML_HEADS, 1, ML_DIM), (2, ML_HEADS, 1, ML_DIM)]
    state_specs = [pl.BlockSpec((1,) + shp, lambda b, c: (b, 0, 0, 0, 0)) for shp in state_shapes]
    if init is not None:
        in_specs += state_specs
        args += list(init)
    out_shape = ([jax.ShapeDtypeStruct((r, width), F32)] * 2
                 + [jax.ShapeDtypeStruct((batch,) + shp, F32) for shp in state_shapes])
    out_specs = [pl.BlockSpec((L, width), lambda b, c: (fwd(b, c), 0)),
                 pl.BlockSpec((L, width), lambda b, c: (bwd(b, c), 0))] + state_specs
    return pl.pallas_call(
        functools.partial(_mlstm_kernel, has_init=init is not None),
        grid=(batch, nc), in_specs=in_specs, out_specs=out_specs, out_shape=out_shape,
        scratch_shapes=[pltpu.VMEM(shp, F32) for shp in state_shapes],
        compiler_params=_params(("arbitrary", "arbitrary")),
        name="mlstm_dec" if init is not None else "mlstm_ctx",
    )(*args)


def _gate(mod_ref, which):
    return mod_ref[0, :, (3 * which + 2) * D_MODEL:(3 * which + 3) * D_MODEL]


def _even_outproj_kernel(da_ref, hf_ref, hb_ref, mo_ref, gn_ref, w_ref, x_ref, mod_ref, o_ref):
    acc = jnp.dot(da_ref[...], w_ref[0:SEG, :], preferred_element_type=F32)
    for hh in range(ML_HEADS):
        sl = slice(hh * ML_DIM, (hh + 1) * ML_DIM)
        y = _rms_rows(hf_ref[:, sl] + hb_ref[:, sl], gn_ref[...]) * mo_ref[:, sl].astype(F32)
        acc += jnp.dot(y.astype(BF16), w_ref[SEG + hh * ML_DIM:SEG + (hh + 1) * ML_DIM, :],
                       preferred_element_type=F32)
    o_ref[...] = x_ref[...] + _gate(mod_ref, 0) * acc


def _even_outproj(da, hf, hb, mo, gain, w, x, mod, rows_per_batch, tm):
    r = x.shape[0]
    tpb = rows_per_batch // tm
    row = lambda i: (i, 0)
    const = lambda i: (0, 0)
    return pl.pallas_call(
        _even_outproj_kernel, grid=(r // tm,),
        in_specs=[pl.BlockSpec((tm, SEG), row)] * 4 + [
            pl.BlockSpec((1, ML_DIM), const), pl.BlockSpec(w.shape, const),
            pl.BlockSpec((tm, D_MODEL), row), pl.BlockSpec((1, 1, 6 * D_MODEL), lambda i: (i // tpb, 0, 0))],
        out_specs=pl.BlockSpec((tm, D_MODEL), row),
        out_shape=jax.ShapeDtypeStruct(x.shape, F32),
        compiler_params=_params(("arbitrary",)), name="even_outproj",
    )(da, hf, hb, mo, gain, w, x, mod)


def _odd_outproj_kernel(a_ref, w_ref, x_ref, mod_ref, o_ref):
    acc = jnp.dot(a_ref[...], w_ref[...], preferred_element_type=F32)
    o_ref[...] = x_ref[...] + _gate(mod_ref, 0) * acc


def _odd_outproj(a, w, x, mod, rows_per_batch, tm):
    r = x.shape[0]
    tpb = rows_per_batch // tm
    row = lambda i: (i, 0)
    return pl.pallas_call(
        _odd_outproj_kernel, grid=(r // tm,),
        in_specs=[pl.BlockSpec((tm, a.shape[1]), row), pl.BlockSpec(w.shape, lambda i: (0, 0)),
                  pl.BlockSpec((tm, D_MODEL), row), pl.BlockSpec((1, 1, 6 * D_MODEL), lambda i: (i // tpb, 0, 0))],
        out_specs=pl.BlockSpec((tm, D_MODEL), row),
        out_shape=jax.ShapeDtypeStruct(x.shape, F32),
        compiler_params=_params(("arbitrary",)), name="odd_outproj",
    )(a, w, x, mod)


def _ffn_kernel(x_ref, mod_ref, n2_ref, wi_ref, wo_ref, o_ref):
    hb = _modulated(x_ref, mod_ref, n2_ref, 1).astype(BF16)
    acc = jnp.zeros(o_ref.shape, F32)
    for c in range(D_FF // FF_CHUNK):
        sl = slice(c * FF_CHUNK, (c + 1) * FF_CHUNK)
        gate = jnp.dot(hb, wi_ref[:, sl], preferred_element_type=F32)
        up = jnp.dot(hb, wi_ref[:, D_FF + c * FF_CHUNK:D_FF + (c + 1) * FF_CHUNK], preferred_element_type=F32)
        act = (gate * _sigmoid(gate) * up).astype(BF16)
        acc += jnp.dot(act, wo_ref[sl, :], preferred_element_type=F32)
    o_ref[...] = x_ref[...] + _gate(mod_ref, 1) * acc


def _ffn(x, mod, norm2, w_in, w_out, rows_per_batch, tm):
    r = x.shape[0]
    tpb = rows_per_batch // tm
    row = lambda i: (i, 0)
    const = lambda i: (0, 0)
    return pl.pallas_call(
        _ffn_kernel, grid=(r // tm,),
        in_specs=[pl.BlockSpec((tm, D_MODEL), row), pl.BlockSpec((1, 1, 6 * D_MODEL), lambda i: (i // tpb, 0, 0)),
                  pl.BlockSpec((1, D_MODEL), const), pl.BlockSpec(w_in.shape, const), pl.BlockSpec(w_out.shape, const)],
        out_specs=pl.BlockSpec((tm, D_MODEL), row),
        out_shape=jax.ShapeDtypeStruct(x.shape, F32),
        compiler_params=_params(("arbitrary",)), name="ffn",
    )(x, mod, norm2, w_in, w_out)


def _rope_tables(n_tokens):
    t = jnp.arange(n_tokens)
    pos = jnp.stack([(t // GRID_W).astype(F32), (t % GRID_W).astype(F32)], axis=1)
    freqs = ROPE_THETA ** (-jnp.arange(ROPE_PAIRS, dtype=F32) / ROPE_PAIRS)
    ang = pos[:, :, None] * freqs
    cos, sin = jnp.cos(ang), jnp.sin(ang)
    zero = jnp.zeros_like(sin)
    lay = lambda first, second: jnp.tile(jnp.stack([first, second], axis=2).reshape(n_tokens, HEAD_DIM), (1, 2))
    return lay(cos, cos), lay(-sin, zero), lay(zero, sin)


def _with_ones(v):
    rows, width = v.shape
    v3 = v.reshape(rows, width // LANES, LANES).astype(BF16)
    return jnp.concatenate([v3, jnp.ones_like(v3)], axis=-1).reshape(rows, 2 * width)


def _trunk(x, mods, li, P, batch, rope_tabs, ctx, tm):
    lj = li // 2
    rows_per_batch = x.shape[0] // batch
    mod_rpb = rows_per_batch if mods.shape[0] > 1 else x.shape[0]
    is_ctx = ctx is None
    new = None
    if li % 2 == 0:
        lam_init = 0.8 - 0.6 * math.exp(-0.3 * li)
        outs = _even_inproj(x, mods, P['norm1'][li], P['w_even_main'][lj], P['b_gate'][lj],
                            P['gq_even'][lj], P['gk_even'][lj], rope_tabs, mod_rpb, is_ctx, tm)
        q, k, v, mq, mk, mv, mo, g, gt = outs[:9]
        ck, cv, init = (None, None, None) if is_ctx else ctx
        if is_ctx:
            da = _da_ctx_attention(q, k, v, P['lam_even'][lj], P['da_norm'][lj], batch, lam_init)
        else:
            da = _pipe_attention("da", q, k, v, ck, cv, batch, ATTN_TQ,
                                 extra=(P['lam_even'][lj], P['da_norm'][lj]), lam_init=lam_init)
        hf, hb, c_n, n_n, m_n = _mlstm(mq, mk, mv, g, gt, init, batch)
        x = _even_outproj(da, hf, hb, mo, P['ml_norm'][lj], P['w_out_even'][lj], x, mods, mod_rpb, tm)
        if is_ctx:
            new = (outs[9], outs[10], c_n, n_n, m_n)
    else:
        outs = _odd_inproj(x, mods, P['norm1'][li], P['w_in_odd'][lj], P['gq_odd'][lj], P['gk_odd'][lj],
                           rope_tabs, mod_rpb, is_ctx, tm)
        q, k, v = outs[:3]
        ck, cv = (None, None) if is_ctx else ctx
        if is_ctx:
            a = _gqa_ctx_attention(q, k, v, batch)
        else:
            a = _pipe_attention("gqa", q, k, v, ck, cv, batch, ATTN_TQ)
        x = _odd_outproj(a, P['w_out_odd'][lj], x, mods, mod_rpb, tm)
        if is_ctx:
            new = (outs[3], outs[4])
    x = _ffn(x, mods, P['norm2'][li], P['w_ffn_in'][li], P['w_ffn_out'][li], mod_rpb, tm)
    return x, new


def kernel(x_prompt, x_sample, c, cache_da_k, cache_da_v, state_mlstm_C, state_mlstm_n, state_mlstm_m, cache_gqa_k, cache_gqa_v, c_ctx, norm1, norm2, w_mod, b_mod, w_in_even, b_gate_even, qk_gain_even, lam_even, da_norm_even, ml_norm_even, w_out_even, w_in_odd, qk_gain_odd, w_out_odd, w_ffn_in, w_ffn_out):
    batch, seq, _ = x_prompt.shape
    dbatch, dseq, _ = x_sample.shape
    depth = norm1.shape[0]
    n_even, n_odd = w_in_even.shape[0], w_in_odd.shape[0]
    past = cache_da_k.shape[2]
    tm = 512

    P = dict(
        norm1=norm1.reshape(depth, 1, D_MODEL), norm2=norm2.reshape(depth, 1, D_MODEL),
        w_even_main=jnp.pad(w_in_even, ((0, 0), (0, 0), (0, LANES - N_GATES))).astype(BF16),
        b_gate=jnp.pad(b_gate_even.reshape(n_even, 1, N_GATES), ((0, 0), (0, 0), (0, LANES - N_GATES))),
        gq_even=qk_gain_even[:, 0].reshape(n_even, 1, LANES), gk_even=qk_gain_even[:, 1].reshape(n_even, 1, LANES),
        lam_even=lam_even, da_norm=da_norm_even.reshape(n_even, 1, LANES),
        ml_norm=ml_norm_even.reshape(n_even, 1, ML_DIM),
        w_out_even=w_out_even.astype(BF16), w_in_odd=w_in_odd.astype(BF16),
        gq_odd=jnp.tile(qk_gain_odd[:, 0], (1, 2)).reshape(n_odd, 1, LANES),
        gk_odd=jnp.tile(qk_gain_odd[:, 1], (1, 2)).reshape(n_odd, 1, LANES),
        w_out_odd=w_out_odd.astype(BF16), w_ffn_in=w_ffn_in.astype(BF16), w_ffn_out=w_ffn_out.astype(BF16))

    cond8 = jnp.concatenate([c_ctx[None, :], c, jnp.zeros((8 - 1 - dbatch, D_MODEL), F32)], axis=0)
    mod = _modulation(cond8, w_mod, b_mod)
    rope_tabs = _rope_tables(dseq)

    x = x_prompt.reshape(batch * seq, D_MODEL)
    ctx_even, ctx_odd = [], []
    for li in range(depth):
        x, new = _trunk(x, mod[li, 0:1].reshape(1, 1, 6 * D_MODEL), li, P, batch, None, None, tm)
        (ctx_even if li % 2 == 0 else ctx_odd).append(new)
    y_prompt = x.reshape(batch, seq, D_MODEL)

    x = x_sample.reshape(dbatch * dseq, D_MODEL)
    for li in range(depth):
        j = li // 2
        if li % 2 == 0:
            ctx = (cache_da_k[:, j].reshape(dbatch * past, SEG).astype(BF16),
                   _with_ones(cache_da_v[:, j].reshape(dbatch * past, SEG)),
                   (state_mlstm_C[:, j], state_mlstm_n[:, j][:, :, :, None, :],
                    jnp.broadcast_to(state_mlstm_m[:, j][:, :, :, None, None], (dbatch, 2, ML_HEADS, 1, ML_DIM))))
        else:
            ctx = (cache_gqa_k[:, j].reshape(dbatch * past, GQA_KV_HEADS * HEAD_DIM).astype(BF16),
                   _with_ones(cache_gqa_v[:, j].reshape(dbatch * past, GQA_KV_HEADS * HEAD_DIM)))
        x, _ = _trunk(x, mod[li, 1:1 + dbatch].reshape(dbatch, 1, 6 * D_MODEL), li, P, dbatch, rope_tabs, ctx, tm)
    y_sample = x.reshape(dbatch, dseq, D_MODEL)

    new_da_k = jnp.stack([t[0].reshape(batch, seq, DA_HEADS, 2, HEAD_DIM) for t in ctx_even], axis=1)
    new_da_v = jnp.stack([t[1].reshape(batch, seq, DA_HEADS, 2 * HEAD_DIM) for t in ctx_even], axis=1)
    new_c = jnp.stack([t[2] for t in ctx_even], axis=1)
    new_n = jnp.stack([t[3][:, :, :, 0, :] for t in ctx_even], axis=1)
    new_m = jnp.stack([t[4][:, :, :, 0, 0] for t in ctx_even], axis=1)
    new_gqa_k = jnp.stack([t[0].reshape(batch, seq, GQA_KV_HEADS, HEAD_DIM) for t in ctx_odd], axis=1)
    new_gqa_v = jnp.stack([t[1].reshape(batch, seq, GQA_KV_HEADS, HEAD_DIM) for t in ctx_odd], axis=1)
    return (y_prompt, y_sample, new_da_k, new_da_v, new_c, new_n, new_m, new_gqa_k, new_gqa_v)
```

```python
import functools
import math

import jax
import jax.numpy as jnp
from jax import lax
from jax.experimental import pallas as pl
from jax.experimental.pallas import tpu as pltpu

F32 = jnp.float32
BF16 = jnp.bfloat16
HIGHEST = lax.Precision.HIGHEST

D_MODEL = 1024
GRID_W = 64
HEAD_DIM = 64
ROPE_PAIRS = HEAD_DIM // 4
ROPE_THETA = 10000.0
EPS = 1e-6
DA_HEADS = 4
ML_HEADS = 4
ML_DIM = 128
N_GATES = 2 * 2 * ML_HEADS
GQA_Q_HEADS = 16
GQA_KV_HEADS = 4
SEG = 512
EVEN_MAIN = 7 * SEG
D_FF = -(-8 * D_MODEL // (3 * 256)) * 256

LANES = 128
VMEM_LIMIT = 56 * 2 ** 20

ATTN_TQ = 256
Q_SCALE = math.log2(math.e) * HEAD_DIM ** -0.5
ML_CHUNK = 128
FF_CHUNK = 256


def _params(sem, vmem=VMEM_LIMIT):
    return pltpu.CompilerParams(dimension_semantics=sem, vmem_limit_bytes=vmem)


def _rms_rows(x, g):
    return x * lax.rsqrt(jnp.mean(x * x, axis=-1, keepdims=True) + EPS) * g


def _lane_lo():
    return lax.broadcasted_iota(jnp.int32, (1, LANES), 1) < HEAD_DIM


def _half_rms(x, gain, lo):
    x2 = x * x
    s_lo = jnp.sum(jnp.where(lo, x2, 0.0), axis=-1, keepdims=True)
    s_hi = jnp.sum(jnp.where(lo, 0.0, x2), axis=-1, keepdims=True)
    r = jnp.where(lo, lax.rsqrt(s_lo * (1.0 / HEAD_DIM) + EPS), lax.rsqrt(s_hi * (1.0 / HEAD_DIM) + EPS))
    return x * r * gain


def _rope(y, cos, sin_a, sin_b):
    return y * cos + pltpu.roll(y, LANES - ROPE_PAIRS, 1) * sin_a + pltpu.roll(y, ROPE_PAIRS, 1) * sin_b


def _sigmoid(x):
    return 1.0 / (1.0 + jnp.exp(-x))


def _store_with_ones(v_o, v):
    ones = jnp.ones((v.shape[0], LANES), BF16)
    for g in range(v.shape[1] // LANES):
        v_o[:, 2 * g * LANES:(2 * g + 1) * LANES] = v[:, g * LANES:(g + 1) * LANES].astype(BF16)
        v_o[:, (2 * g + 1) * LANES:(2 * g + 2) * LANES] = ones


def _mod_kernel(cond_ref, w_ref, b_ref, o_ref):
    c = cond_ref[...]
    a = c * _sigmoid(c)
    o_ref[0] = jnp.dot(a, w_ref[0], preferred_element_type=F32, precision=HIGHEST) + b_ref[0]


def _modulation(cond8, w_mod, b_mod):
    depth, _, n = w_mod.shape
    tn = 1024
    return pl.pallas_call(
        _mod_kernel,
        grid=(depth, n // tn),
        in_specs=[pl.BlockSpec((8, D_MODEL), lambda l, j: (0, 0)),
                  pl.BlockSpec((1, D_MODEL, tn), lambda l, j: (l, 0, j)),
                  pl.BlockSpec((1, 1, tn), lambda l, j: (l, 0, j))],
        out_specs=pl.BlockSpec((1, 8, tn), lambda l, j: (l, 0, j)),
        out_shape=jax.ShapeDtypeStruct((depth, 8, n), F32),
        compiler_params=_params(("arbitrary", "arbitrary")), name="modulation",
    )(cond8, w_mod, b_mod.reshape(depth, 1, n))


def _modulated(x_ref, mod_ref, norm_ref, which):
    sh = mod_ref[0, :, (3 * which) * D_MODEL:(3 * which + 1) * D_MODEL]
    sc = mod_ref[0, :, (3 * which + 1) * D_MODEL:(3 * which + 2) * D_MODEL]
    return _rms_rows(x_ref[...], norm_ref[...]) * (1.0 + sc) + sh


def _even_inproj_kernel(*refs, rope, cache):
    it = iter(refs)
    x_ref, mod_ref, n1_ref, w_ref, bg_ref, gq_ref, gk_ref = [next(it) for _ in range(7)]
    if rope:
        cos_ref, sa_ref, sb_ref = [next(it) for _ in range(3)]
    q_o, k_o, v_o, mq_o, mk_o, mv_o, mo_o, gt_o = [next(it) for _ in range(8)]
    if cache:
        kf_o, vf_o = [next(it) for _ in range(2)]

    hb = _modulated(x_ref, mod_ref, n1_ref, 0).astype(BF16)
    lo = _lane_lo()

    def seg(i):
        return jnp.dot(hb, w_ref[:, i * SEG:(i + 1) * SEG], preferred_element_type=F32)

    def qk_norm(p, gain_ref, hh):
        y = _half_rms(p[:, hh * LANES:(hh + 1) * LANES], gain_ref[...], lo)
        if rope:
            y = _rope(y, cos_ref[...], sa_ref[...], sb_ref[...])
        return y

    p = seg(0)
    for hh in range(DA_HEADS):
        q_o[:, hh * LANES:(hh + 1) * LANES] = (qk_norm(p, gq_ref, hh) * Q_SCALE).astype(BF16)
    p = seg(1)
    for hh in range(DA_HEADS):
        sl = slice(hh * LANES, (hh + 1) * LANES)
        if cache:
            y = _half_rms(p[:, sl], gk_ref[...], lo)
            kf_o[:, sl] = y
        else:
            y = qk_norm(p, gk_ref, hh)
        k_o[:, sl] = y.astype(BF16)
    p = seg(2)
    if cache:
        v_o[...] = p.astype(BF16)
        vf_o[...] = p
    else:
        _store_with_ones(v_o, p)
    mq_o[...] = seg(3).astype(BF16)
    mk_o[...] = (seg(4) * (ML_DIM ** -0.5)).astype(BF16)
    mv_o[...] = seg(5).astype(BF16)
    mo_o[...] = _sigmoid(seg(6)).astype(BF16)
    pg = jnp.dot(hb, w_ref[:, EVEN_MAIN:EVEN_MAIN + LANES], preferred_element_type=F32) + bg_ref[...]
    gt_o[...] = pg.T[:N_GATES, :]


def _even_inproj(x, mod, norm1, w_main, b_gate, gain_q, gain_k, rope_tabs, rows_per_batch, cache, tm):
    r = x.shape[0]
    nt = r // tm
    tpb = rows_per_batch // tm
    row = lambda i: (i, 0)
    const = lambda i: (0, 0)
    in_specs = [pl.BlockSpec((tm, D_MODEL), row),
                pl.BlockSpec((1, 1, 6 * D_MODEL), lambda i: (i // tpb, 0, 0)),
                pl.BlockSpec((1, D_MODEL), const),
                pl.BlockSpec((D_MODEL, EVEN_MAIN + LANES), const),
                pl.BlockSpec((1, LANES), const),
                pl.BlockSpec((1, LANES), const),
                pl.BlockSpec((1, LANES), const)]
    args = [x, mod, norm1, w_main, b_gate, gain_q, gain_k]
    if rope_tabs is not None:
        in_specs += [pl.BlockSpec((tm, LANES), lambda i: (i % tpb, 0))] * 3
        args += list(rope_tabs)
    out_shape = [jax.ShapeDtypeStruct((r, SEG), BF16)] * 7
    out_specs = [pl.BlockSpec((tm, SEG), row)] * 7
    if not cache:
        out_shape[2] = jax.ShapeDtypeStruct((r, 2 * SEG), BF16)
        out_specs[2] = pl.BlockSpec((tm, 2 * SEG), row)
    out_shape += [jax.ShapeDtypeStruct((N_GATES, r), F32)]
    out_specs += [pl.BlockSpec((N_GATES, tm), lambda i: (0, i))]
    if cache:
        out_shape += [jax.ShapeDtypeStruct((r, SEG), F32)] * 2
        out_specs += [pl.BlockSpec((tm, SEG), row)] * 2
    return pl.pallas_call(
        functools.partial(_even_inproj_kernel, rope=rope_tabs is not None, cache=cache),
        grid=(nt,), in_specs=in_specs, out_specs=out_specs, out_shape=out_shape,
        compiler_params=_params(("arbitrary",)), name="even_inproj_ctx" if cache else "even_inproj_dec",
    )(*args)


def _odd_inproj_kernel(*refs, rope, cache):
    it = iter(refs)
    x_ref, mod_ref, n1_ref, w_ref, gq_ref, gk_ref = [next(it) for _ in range(6)]
    if rope:
        cos_ref, sa_ref, sb_ref = [next(it) for _ in range(3)]
    q_o, k_o, v_o = [next(it) for _ in range(3)]
    if cache:
        kf_o, vf_o = [next(it) for _ in range(2)]

    hb = _modulated(x_ref, mod_ref, n1_ref, 0).astype(BF16)
    lo = _lane_lo()
    nq = GQA_Q_HEADS * HEAD_DIM
    nkv = GQA_KV_HEADS * HEAD_DIM

    def qk_norm(t, gain_ref):
        y = _half_rms(t, gain_ref[...], lo)
        if rope:
            y = _rope(y, cos_ref[...], sa_ref[...], sb_ref[...])
        return y

    for c in range(nq // SEG):
        p = jnp.dot(hb, w_ref[:, c * SEG:(c + 1) * SEG], preferred_element_type=F32)
        for hh in range(SEG // LANES):
            sl = slice(c * SEG + hh * LANES, c * SEG + (hh + 1) * LANES)
            q_o[:, sl] = (qk_norm(p[:, hh * LANES:(hh + 1) * LANES], gq_ref) * Q_SCALE).astype(BF16)
    p = jnp.dot(hb, w_ref[:, nq:nq + 2 * nkv], preferred_element_type=F32)
    for hh in range(nkv // LANES):
        sl = slice(hh * LANES, (hh + 1) * LANES)
        if cache:
            y = _half_rms(p[:, sl], gk_ref[...], lo)
            kf_o[:, sl] = y
        else:
            y = qk_norm(p[:, sl], gk_ref)
        k_o[:, sl] = y.astype(BF16)
    v = p[:, nkv:2 * nkv]
    if cache:
        v_o[...] = v.astype(BF16)
        vf_o[...] = v
    else:
        _store_with_ones(v_o, v)


def _odd_inproj(x, mod, norm1, w, gain_q, gain_k, rope_tabs, rows_per_batch, cache, tm):
    r = x.shape[0]
    nt = r // tm
    tpb = rows_per_batch // tm
    nq = GQA_Q_HEADS * HEAD_DIM
    nkv = GQA_KV_HEADS * HEAD_DIM
    row = lambda i: (i, 0)
    const = lambda i: (0, 0)
    in_specs = [pl.BlockSpec((tm, D_MODEL), row),
                pl.BlockSpec((1, 1, 6 * D_MODEL), lambda i: (i // tpb, 0, 0)),
                pl.BlockSpec((1, D_MODEL), const),
                pl.BlockSpec((D_MODEL, nq + 2 * nkv), const),
                pl.BlockSpec((1, LANES), const),
                pl.BlockSpec((1, LANES), const)]
    args = [x, mod, norm1, w, gain_q, gain_k]
    if rope_tabs is not None:
        in_specs += [pl.BlockSpec((tm, LANES), lambda i: (i % tpb, 0))] * 3
        args += list(rope_tabs)
    out_shape = [jax.ShapeDtypeStruct((r, nq), BF16), jax.ShapeDtypeStruct((r, nkv), BF16),
                 jax.ShapeDtypeStruct((r, nkv), BF16)]
    out_specs = [pl.BlockSpec((tm, nq), row), pl.BlockSpec((tm, nkv), row), pl.BlockSpec((tm, nkv), row)]
    if not cache:
        out_shape[2] = jax.ShapeDtypeStruct((r, 2 * nkv), BF16)
        out_specs[2] = pl.BlockSpec((tm, 2 * nkv), row)
    if cache:
        out_shape += [jax.ShapeDtypeStruct((r, nkv), F32)] * 2
        out_specs += [pl.BlockSpec((tm, nkv), row)] * 2
    return pl.pallas_call(
        functools.partial(_odd_inproj_kernel, rope=rope_tabs is not None, cache=cache),
        grid=(nt,), in_specs=in_specs, out_specs=out_specs, out_shape=out_shape,
        compiler_params=_params(("arbitrary",)), name="odd_inproj_ctx" if cache else "odd_inproj_dec",
    )(*args)


_NT = (((1,), (1,)), ((), ()))


def _attend(qs, sources):
    outs = []
    for q in qs:
        ss = [lax.dot_general(q, k_ref[...], _NT, preferred_element_type=F32) for k_ref, _ in sources]
        m = functools.reduce(jnp.maximum, [jnp.max(s, axis=-1, keepdims=True) for s in ss])
        ps = [jnp.exp2(s - m) for s in ss]
        l = functools.reduce(jnp.add, [jnp.sum(p, axis=-1, keepdims=True) for p in ps])
        acc = functools.reduce(jnp.add, [jnp.dot(p.astype(BF16), v_ref[...], preferred_element_type=F32)
                                         for p, (_, v_ref) in zip(ps, sources)])
        outs.append(acc / l)
    return outs


def _lam_full(lam_ref, lam_init):
    lam = lam_ref[...]
    return (jnp.exp(jnp.sum(lam[0:1] * lam[1:2], axis=-1, keepdims=True))
            - jnp.exp(jnp.sum(lam[2:3] * lam[3:4], axis=-1, keepdims=True)) + lam_init)


def _da_ctx_kernel(q_ref, k_ref, v_ref, lam_ref, gn_ref, o_ref, *, lam_init):
    lo = _lane_lo()
    lam_full = _lam_full(lam_ref, lam_init)
    for h in range(DA_HEADS):
        sl = slice(h * LANES, (h + 1) * LANES)
        q = q_ref[:, sl]
        zero = jnp.zeros_like(q)
        a1, a2 = _attend([jnp.where(lo, q, zero), jnp.where(lo, zero, q)], [(k_ref.at[:, sl], v_ref.at[:, sl])])
        o = a1 - lam_full * a2
        o_ref[:, sl] = (_rms_rows(o, gn_ref[...]) * (1.0 - lam_init)).astype(BF16)


def _da_ctx_attention(q, k, v, lam, gain, batch, lam_init):
    r = q.shape[0]
    s = r // batch
    blk = pl.BlockSpec((s, SEG), lambda b: (b, 0))
    const = lambda b: (0, 0)
    return pl.pallas_call(
        functools.partial(_da_ctx_kernel, lam_init=lam_init), grid=(batch,),
        in_specs=[blk, blk, blk, pl.BlockSpec(lam.shape, const), pl.BlockSpec((1, LANES), const)],
        out_specs=blk, out_shape=jax.ShapeDtypeStruct((r, SEG), BF16),
        compiler_params=_params(("arbitrary",)), name="da_attn_ctx",
    )(q, k, v, lam, gain)


def _gqa_ctx_kernel(q_ref, k_ref, v_ref, o_ref):
    lo = _lane_lo()
    group_pairs = GQA_Q_HEADS // GQA_KV_HEADS // 2
    for j in range(GQA_KV_HEADS // 2):
        kv = (k_ref.at[:, j * LANES:(j + 1) * LANES], v_ref.at[:, j * LANES:(j + 1) * LANES])
        for half in range(2):
            keep = lo if half == 0 else jnp.logical_not(lo)
            for pr in range(group_pairs):
                col = ((j * 2 + half) * group_pairs + pr) * LANES
                qp = q_ref[:, col:col + LANES].astype(F32)
                qr = pltpu.roll(qp, HEAD_DIM, 1)
                qa = jnp.where(keep, jnp.where(lo, qp, qr), 0.0).astype(BF16)
                qb = jnp.where(keep, jnp.where(lo, qr, qp), 0.0).astype(BF16)
                oa, ob = _attend([qa, qb], [kv])
                first = jnp.where(keep, oa, pltpu.roll(oa, HEAD_DIM, 1))
                second = jnp.where(keep, ob, pltpu.roll(ob, HEAD_DIM, 1))
                o_ref[:, col:col + LANES] = jnp.where(lo, first, second).astype(BF16)


def _gqa_ctx_attention(q, k, v, batch):
    r = q.shape[0]
    s = r // batch
    row = lambda b: (b, 0)
    return pl.pallas_call(
        _gqa_ctx_kernel, grid=(batch,),
        in_specs=[pl.BlockSpec((s, q.shape[1]), row), pl.BlockSpec((s, k.shape[1]), row),
                  pl.BlockSpec((s, v.shape[1]), row)],
        out_specs=pl.BlockSpec((s, q.shape[1]), row), out_shape=jax.ShapeDtypeStruct(q.shape, BF16),
        compiler_params=_params(("arbitrary",)), name="gqa_attn_ctx",
    )(q, k, v)


def _score_item(qs, k_ref, ck_ref, s_w, m_w):
    nk, past = k_ref.shape[0], ck_ref.shape[0]
    for a, q in enumerate(qs):
        s1 = lax.dot_general(q, k_ref[...], _NT, preferred_element_type=F32)
        s2 = lax.dot_general(q, ck_ref[...], _NT, preferred_element_type=F32)
        s_w[a, :, 0:nk] = s1
        s_w[a, :, nk:nk + past] = s2
        m_w[a] = jnp.maximum(jnp.max(s1, axis=-1, keepdims=True), jnp.max(s2, axis=-1, keepdims=True))


def _finish_item(v_ref, cv_ref, s_r, m_r):
    nk, past = v_ref.shape[0], cv_ref.shape[0]
    outs = []
    for a in range(2):
        m = m_r[a]
        p1 = jnp.exp2(s_r[a, :, 0:nk] - m).astype(BF16)
        p2 = jnp.exp2(s_r[a, :, nk:nk + past] - m).astype(BF16)
        acc = (jnp.dot(p1, v_ref[...], preferred_element_type=F32)
               + jnp.dot(p2, cv_ref[...], preferred_element_type=F32))
        outs.append(acc[:, 0:LANES] / acc[:, LANES:LANES + 1])
    return outs


def _by_parity(t, body, sa, ma, sb, mb):
    @pl.when(t == 0)
    def _():
        sb[...] = jnp.zeros(sb.shape, F32)
        mb[...] = jnp.zeros(mb.shape, F32)

    @pl.when(t % 2 == 0)
    def _():
        body(sa, ma, sb, mb)

    @pl.when(t % 2 == 1)
    def _():
        body(sb, mb, sa, ma)


def _da_pipe_kernel(q_ref, k_ref, ck_ref, v_ref, cv_ref, lam_ref, gn_ref, o_ref, sa, ma, sb, mb, *, lam_init):
    lo = _lane_lo()

    def body(s_w, m_w, s_r, m_r):
        q = q_ref[...]
        zero = jnp.zeros_like(q)
        _score_item([jnp.where(lo, q, zero), jnp.where(lo, zero, q)], k_ref, ck_ref, s_w, m_w)
        a1, a2 = _finish_item(v_ref, cv_ref, s_r, m_r)
        o = a1 - _lam_full(lam_ref, lam_init) * a2
        o_ref[...] = (_rms_rows(o, gn_ref[...]) * (1.0 - lam_init)).astype(BF16)

    _by_parity(pl.program_id(0), body, sa, ma, sb, mb)


def _gqa_pipe_kernel(q_ref, k_ref, ck_ref, v_ref, cv_ref, o_ref, sa, ma, sb, mb, *, n_items):
    t = pl.program_id(0)
    lo = _lane_lo()
    q_pairs = GQA_Q_HEADS // GQA_KV_HEADS

    def keep_of(item):
        half = (item % q_pairs) // (q_pairs // 2)
        return jnp.where(lo, 0, 1) == half

    def body(s_w, m_w, s_r, m_r):
        keep = keep_of(jnp.minimum(t, n_items - 1))
        qp = q_ref[...].astype(F32)
        qr = pltpu.roll(qp, HEAD_DIM, 1)
        qa = jnp.where(keep, jnp.where(lo, qp, qr), 0.0).astype(BF16)
        qb = jnp.where(keep, jnp.where(lo, qr, qp), 0.0).astype(BF16)
        _score_item([qa, qb], k_ref, ck_ref, s_w, m_w)
        keep = keep_of(jnp.maximum(t - 1, 0))
        oa, ob = _finish_item(v_ref, cv_ref, s_r, m_r)
        first = jnp.where(keep, oa, pltpu.roll(oa, HEAD_DIM, 1))
        second = jnp.where(keep, ob, pltpu.roll(ob, HEAD_DIM, 1))
        o_ref[...] = jnp.where(lo, first, second).astype(BF16)

    _by_parity(t, body, sa, ma, sb, mb)


def _pipe_attention(kind, q, k, v1, ck, cv1, batch, tq, extra=(), lam_init=None):
    r = q.shape[0]
    s = r // batch
    past = ck.shape[0] // batch
    nq = s // tq
    groups = k.shape[1] // LANES
    per_q = q.shape[1] // LANES // groups
    per_g = nq * per_q
    n_items = batch * groups * per_g

    def split(item):
        b, g, rest = item // (groups * per_g), (item // per_g) % groups, item % per_g
        return b, g, rest // per_q, rest % per_q

    def q_index(item):
        b, g, i, p = split(item)
        return (b * nq + i, g * per_q + p)

    def kv_index(item):
        b, g, _, _ = split(item)
        return (b, g)

    cur = lambda t: jnp.minimum(t, n_items - 1)
    prev = lambda t: jnp.maximum(t - 1, 0)
    in_specs = [pl.BlockSpec((tq, LANES), lambda t: q_index(cur(t))),
                pl.BlockSpec((s, LANES), lambda t: kv_index(cur(t))),
                pl.BlockSpec((past, LANES), lambda t: kv_index(cur(t))),
                pl.BlockSpec((s, 2 * LANES), lambda t: kv_index(prev(t))),
                pl.BlockSpec((past, 2 * LANES), lambda t: kv_index(prev(t)))]
    in_specs += [pl.BlockSpec(e.shape, lambda t: (0, 0)) for e in extra]
    if kind == "da":
        body = functools.partial(_da_pipe_kernel, lam_init=lam_init)
    else:
        body = functools.partial(_gqa_pipe_kernel, n_items=n_items)
    scratch = [pltpu.VMEM((2, tq, s + past), F32), pltpu.VMEM((2, tq, 1), F32)] * 2
    return pl.pallas_call(
        body, grid=(n_items + 1,), in_specs=in_specs,
        out_specs=pl.BlockSpec((tq, LANES), lambda t: q_index(prev(t))),
        out_shape=jax.ShapeDtypeStruct(q.shape, BF16),
        scratch_shapes=scratch,
        compiler_params=_params(("arbitrary",)),
        name=kind + "_attn_dec",
    )(q, k, ck, v1, cv1, *extra)


def _log_sigmoid(x):
    return jnp.minimum(x, 0.0) - jnp.log1p(jnp.exp(-jnp.abs(x)))


ML_PAD = 8


def _mlstm_chunk(q, k, v, gt, c_s, n_s, m_s, head, direction):
    L, D = ML_CHUNK, ML_DIM
    i_idx = direction * 2 * ML_HEADS + head
    f_idx = i_idx + ML_HEADS
    i_row, f_row = gt[i_idx:i_idx + 1, :], gt[f_idx:f_idx + 1, :]
    lf_row = _log_sigmoid(f_row)
    r_i = lax.broadcasted_iota(jnp.int32, (L, L), 0)
    c_i = lax.broadcasted_iota(jnp.int32, (L, L), 1)
    vis = (r_i <= c_i) if direction == 0 else (r_i >= c_i)
    between = (c_i > r_i) if direction == 0 else (c_i < r_i)
    lhs_terms, lf_rem, i_rem = [], lf_row, i_row
    for _ in range(3):
        lf_t, i_t = lf_rem.astype(BF16), i_rem.astype(BF16)
        lf_rem, i_rem = lf_rem - lf_t.astype(F32), i_rem - i_t.astype(F32)
        lf_f, i_f = lf_t.astype(F32), i_t.astype(F32)
        lhs_terms.append(jnp.concatenate([
            jnp.concatenate([jnp.where(between, lf_f, 0.0), jnp.where(r_i == c_i, i_f, 0.0)], axis=1),
            jnp.concatenate([jnp.broadcast_to(lf_f, (ML_PAD, L)), jnp.zeros((ML_PAD, L), F32)], axis=1)],
            axis=0).astype(BF16))
    rhs = jnp.concatenate([jnp.where(vis, 1.0, 0.0), jnp.ones((L, L), F32)], axis=0).astype(BF16)
    d_all = jnp.dot(jnp.concatenate(lhs_terms, axis=1), jnp.concatenate([rhs] * 3, axis=0),
                    preferred_element_type=F32)
    b_row = d_all[L:L + 1, :]
    last = slice(L - 1, L) if direction == 0 else slice(0, 1)
    b_last = b_row[:, last]
    m_prev = m_s[:, 0:1]
    dmat = jnp.where(vis, d_all[0:L, :], -jnp.inf)
    inter = b_row + m_prev
    m_t = jnp.maximum(inter, jnp.max(dmat, axis=0, keepdims=True))
    w_intra = jnp.exp(dmat - m_t)
    w_inter = jnp.exp(inter - m_t)
    a_t = (lax.dot_general(k, q, _NT, preferred_element_type=F32) * w_intra).astype(BF16)
    v_t = v.astype(F32).T
    v1 = jnp.concatenate([v_t, jnp.ones((ML_PAD, L), F32)], axis=0).astype(BF16)
    state = jnp.concatenate([c_s[...], jnp.broadcast_to(n_s[...], (ML_PAD, D))], axis=0)
    tot = (jnp.dot(v1, a_t, preferred_element_type=F32)
           + w_inter * lax.dot_general(state.astype(BF16), q, _NT, preferred_element_type=F32))
    h_t = tot[0:D, :] / jnp.maximum(jnp.abs(tot[D:D + 1, :]), jnp.exp(-m_t))
    m_new = m_t[:, last]
    w_s = jnp.exp(b_last - b_row + i_row - m_new)
    decay = jnp.exp(b_last + m_prev - m_new)
    vw = jnp.concatenate([v_t * w_s, jnp.broadcast_to(w_s, (ML_PAD, L))], axis=0).astype(BF16)
    state = decay * state + jnp.dot(vw, k, preferred_element_type=F32)
    return h_t.T, state[0:D, :], state[D:D + 1, :], jnp.broadcast_to(m_new, (1, D))


def _mlstm_kernel(*refs, has_init):
    it = iter(refs)
    sides = [[next(it) for _ in range(4)] for _ in range(2)]
    if has_init:
        c0, n0, m0 = [next(it) for _ in range(3)]
    h_os = [next(it) for _ in range(2)]
    c_o, n_o, m_o = [next(it) for _ in range(3)]
    c_s, n_s, m_s = [next(it) for _ in range(3)]
    chunk = pl.program_id(1)

    @pl.when(chunk == 0)
    def _():
        if has_init:
            c_s[...] = c0[0]
            n_s[...] = n0[0]
            m_s[...] = m0[0]
        else:
            for ref in (c_s, n_s, m_s):
                ref[...] = jnp.zeros(ref.shape, F32)

    new_c, new_n, new_m = [], [], []
    for d, ((q_ref, k_ref, v_ref, gt_ref), h_o) in enumerate(zip(sides, h_os)):
        gt = gt_ref[...]
        hs = []
        for h in range(ML_HEADS):
            cols = slice(h * ML_DIM, (h + 1) * ML_DIM)
            h_new, c_new, n_new, m_new = _mlstm_chunk(q_ref[:, cols], k_ref[:, cols], v_ref[:, cols], gt,
                                                      c_s.at[d, h], n_s.at[d, h], m_s.at[d, h], h, d)
            hs.append(h_new)
            new_c.append(c_new)
            new_n.append(n_new)
            new_m.append(m_new)
        h_o[...] = jnp.concatenate(hs, axis=1)
    c_s[...] = jnp.stack(new_c).reshape(c_s.shape)
    n_s[...] = jnp.stack(new_n).reshape(n_s.shape)
    m_s[...] = jnp.stack(new_m).reshape(m_s.shape)

    @pl.when(chunk == pl.num_programs(1) - 1)
    def _():
        c_o[0] = c_s[...]
        n_o[0] = n_s[...]
        m_o[0] = m_s[...]


def _mlstm(q, k, v, gt, init, batch):
    r = q.shape[0]
    L = ML_CHUNK
    nc = r // batch // L
    width = ML_HEADS * ML_DIM
    fwd = lambda b, c: b * nc + c
    bwd = lambda b, c: b * nc + nc - 1 - c

    def side(pos):
        return ([pl.BlockSpec((L, width), lambda b, c: (pos(b, c), 0))] * 3
                + [pl.BlockSpec((N_GATES, L), lambda b, c: (0, pos(b, c)))])

    in_specs = side(fwd) + side(bwd)
    args = [q, k, v, gt] * 2
    state_shapes = [(2, ML_HEADS, ML_DIM, ML_DIM), (2, ML_HEADS, 1, ML_DIM), (2, ML_HEADS, 1, ML_DIM)]
    state_specs = [pl.BlockSpec((1,) + shp, lambda b, c: (b, 0, 0, 0, 0)) for shp in state_shapes]
    if init is not None:
        in_specs += state_specs
        args += list(init)
    out_shape = ([jax.ShapeDtypeStruct((r, width), F32)] * 2
                 + [jax.ShapeDtypeStruct((batch,) + shp, F32) for shp in state_shapes])
    out_specs = [pl.BlockSpec((L, width), lambda b, c: (fwd(b, c), 0)),
                 pl.BlockSpec((L, width), lambda b, c: (bwd(b, c), 0))] + state_specs
    return pl.pallas_call(
        functools.partial(_mlstm_kernel, has_init=init is not None),
        grid=(batch, nc), in_specs=in_specs, out_specs=out_specs, out_shape=out_shape,
        scratch_shapes=[pltpu.VMEM(shp, F32) for shp in state_shapes],
        compiler_params=_params(("arbitrary", "arbitrary")),
        name="mlstm_dec" if init is not None else "mlstm_ctx",
    )(*args)


def _gate(mod_ref, which):
    return mod_ref[0, :, (3 * which + 2) * D_MODEL:(3 * which + 3) * D_MODEL]


def _even_outproj_kernel(da_ref, hf_ref, hb_ref, mo_ref, gn_ref, w_ref, x_ref, mod_ref, o_ref):
    acc = jnp.dot(da_ref[...], w_ref[0:SEG, :], preferred_element_type=F32)
    for hh in range(ML_HEADS):
        sl = slice(hh * ML_DIM, (hh + 1) * ML_DIM)
        y = _rms_rows(hf_ref[:, sl] + hb_ref[:, sl], gn_ref[...]) * mo_ref[:, sl].astype(F32)
        acc += jnp.dot(y.astype(BF16), w_ref[SEG + hh * ML_DIM:SEG + (hh + 1) * ML_DIM, :],
                       preferred_element_type=F32)
    o_ref[...] = x_ref[...] + _gate(mod_ref, 0) * acc


def _even_outproj(da, hf, hb, mo, gain, w, x, mod, rows_per_batch, tm):
    r = x.shape[0]
    tpb = rows_per_batch // tm
    row = lambda i: (i, 0)
    const = lambda i: (0, 0)
    return pl.pallas_call(
        _even_outproj_kernel, grid=(r // tm,),
        in_specs=[pl.BlockSpec((tm, SEG), row)] * 4 + [
            pl.BlockSpec((1, ML_DIM), const), pl.BlockSpec(w.shape, const),
            pl.BlockSpec((tm, D_MODEL), row), pl.BlockSpec((1, 1, 6 * D_MODEL), lambda i: (i // tpb, 0, 0))],
        out_specs=pl.BlockSpec((tm, D_MODEL), row),
        out_shape=jax.ShapeDtypeStruct(x.shape, F32),
        compiler_params=_params(("arbitrary",)), name="even_outproj",
    )(da, hf, hb, mo, gain, w, x, mod)


def _odd_outproj_kernel(a_ref, w_ref, x_ref, mod_ref, o_ref):
    acc = jnp.dot(a_ref[...], w_ref[...], preferred_element_type=F32)
    o_ref[...] = x_ref[...] + _gate(mod_ref, 0) * acc


def _odd_outproj(a, w, x, mod, rows_per_batch, tm):
    r = x.shape[0]
    tpb = rows_per_batch // tm
    row = lambda i: (i, 0)
    return pl.pallas_call(
        _odd_outproj_kernel, grid=(r // tm,),
        in_specs=[pl.BlockSpec((tm, a.shape[1]), row), pl.BlockSpec(w.shape, lambda i: (0, 0)),
                  pl.BlockSpec((tm, D_MODEL), row), pl.BlockSpec((1, 1, 6 * D_MODEL), lambda i: (i // tpb, 0, 0))],
        out_specs=pl.BlockSpec((tm, D_MODEL), row),
        out_shape=jax.ShapeDtypeStruct(x.shape, F32),
        compiler_params=_params(("arbitrary",)), name="odd_outproj",
    )(a, w, x, mod)


def _ffn_kernel(x_ref, mod_ref, n2_ref, wi_ref, wo_ref, o_ref):
    hb = _modulated(x_ref, mod_ref, n2_ref, 1).astype(BF16)
    acc = jnp.zeros(o_ref.shape, F32)
    for c in range(D_FF // FF_CHUNK):
        sl = slice(c * FF_CHUNK, (c + 1) * FF_CHUNK)
        gate = jnp.dot(hb, wi_ref[:, sl], preferred_element_type=F32)
        up = jnp.dot(hb, wi_ref[:, D_FF + c * FF_CHUNK:D_FF + (c + 1) * FF_CHUNK], preferred_element_type=F32)
        act = (gate * _sigmoid(gate) * up).astype(BF16)
        acc += jnp.dot(act, wo_ref[sl, :], preferred_element_type=F32)
    o_ref[...] = x_ref[...] + _gate(mod_ref, 1) * acc


def _ffn(x, mod, norm2, w_in, w_out, rows_per_batch, tm):
    r = x.shape[0]
    tpb = rows_per_batch // tm
    row = lambda i: (i, 0)
    const = lambda i: (0, 0)
    return pl.pallas_call(
        _ffn_kernel, grid=(r // tm,),
        in_specs=[pl.BlockSpec((tm, D_MODEL), row), pl.BlockSpec((1, 1, 6 * D_MODEL), lambda i: (i // tpb, 0, 0)),
                  pl.BlockSpec((1, D_MODEL), const), pl.BlockSpec(w_in.shape, const), pl.BlockSpec(w_out.shape, const)],
        out_specs=pl.BlockSpec((tm, D_MODEL), row),
        out_shape=jax.ShapeDtypeStruct(x.shape, F32),
        compiler_params=_params(("arbitrary",)), name="ffn",
    )(x, mod, norm2, w_in, w_out)


def _rope_tables(n_tokens):
    t = jnp.arange(n_tokens)
    pos = jnp.stack([(t // GRID_W).astype(F32), (t % GRID_W).astype(F32)], axis=1)
    freqs = ROPE_THETA ** (-jnp.arange(ROPE_PAIRS, dtype=F32) / ROPE_PAIRS)
    ang = pos[:, :, None] * freqs
    cos, sin = jnp.cos(ang), jnp.sin(ang)
    zero = jnp.zeros_like(sin)
    lay = lambda first, second: jnp.tile(jnp.stack([first, second], axis=2).reshape(n_tokens, HEAD_DIM), (1, 2))
    return lay(cos, cos), lay(-sin, zero), lay(zero, sin)


def _with_ones(v):
    rows, width = v.shape
    v3 = v.reshape(rows, width // LANES, LANES).astype(BF16)
    return jnp.concatenate([v3, jnp.ones_like(v3)], axis=-1).reshape(rows, 2 * width)


def _trunk(x, mods, li, P, batch, rope_tabs, ctx, tm):
    lj = li // 2
    rows_per_batch = x.shape[0] // batch
    mod_rpb = rows_per_batch if mods.shape[0] > 1 else x.shape[0]
    is_ctx = ctx is None
    new = None
    if li % 2 == 0:
        lam_init = 0.8 - 0.6 * math.exp(-0.3 * li)
        outs = _even_inproj(x, mods, P['norm1'][li], P['w_even_main'][lj], P['b_gate'][lj],
                            P['gq_even'][lj], P['gk_even'][lj], rope_tabs, mod_rpb, is_ctx, tm)
        q, k, v, mq, mk, mv, mo, gt = outs[:8]
        ck, cv, init = (None, None, None) if is_ctx else ctx
        if is_ctx:
            da = _da_ctx_attention(q, k, v, P['lam_even'][lj], P['da_norm'][lj], batch, lam_init)
        else:
            da = _pipe_attention("da", q, k, v, ck, cv, batch, ATTN_TQ,
                                 extra=(P['lam_even'][lj], P['da_norm'][lj]), lam_init=lam_init)
        hf, hb, c_n, n_n, m_n = _mlstm(mq, mk, mv, gt, init, batch)
        x = _even_outproj(da, hf, hb, mo, P['ml_norm'][lj], P['w_out_even'][lj], x, mods, mod_rpb, tm)
        if is_ctx:
            new = (outs[8], outs[9], c_n, n_n, m_n)
    else:
        outs = _odd_inproj(x, mods, P['norm1'][li], P['w_in_odd'][lj], P['gq_odd'][lj], P['gk_odd'][lj],
                           rope_tabs, mod_rpb, is_ctx, tm)
        q, k, v = outs[:3]
        ck, cv = (None, None) if is_ctx else ctx
        if is_ctx:
            a = _gqa_ctx_attention(q, k, v, batch)
        else:
            a = _pipe_attention("gqa", q, k, v, ck, cv, batch, ATTN_TQ)
        x = _odd_outproj(a, P['w_out_odd'][lj], x, mods, mod_rpb, tm)
        if is_ctx:
            new = (outs[3], outs[4])
    x = _ffn(x, mods, P['norm2'][li], P['w_ffn_in'][li], P['w_ffn_out'][li], mod_rpb, tm)
    return x, new


def kernel(x_prompt, x_sample, c, cache_da_k, cache_da_v, state_mlstm_C, state_mlstm_n, state_mlstm_m, cache_gqa_k, cache_gqa_v, c_ctx, norm1, norm2, w_mod, b_mod, w_in_even, b_gate_even, qk_gain_even, lam_even, da_norm_even, ml_norm_even, w_out_even, w_in_odd, qk_gain_odd, w_out_odd, w_ffn_in, w_ffn_out):
    batch, seq, _ = x_prompt.shape
    dbatch, dseq, _ = x_sample.shape
    depth = norm1.shape[0]
    n_even, n_odd = w_in_even.shape[0], w_in_odd.shape[0]
    past = cache_da_k.shape[2]
    tm = 512

    P = dict(
        norm1=norm1.reshape(depth, 1, D_MODEL), norm2=norm2.reshape(depth, 1, D_MODEL),
        w_even_main=jnp.pad(w_in_even, ((0, 0), (0, 0), (0, LANES - N_GATES))).astype(BF16),
        b_gate=jnp.pad(b_gate_even.reshape(n_even, 1, N_GATES), ((0, 0), (0, 0), (0, LANES - N_GATES))),
        gq_even=qk_gain_even[:, 0].reshape(n_even, 1, LANES), gk_even=qk_gain_even[:, 1].reshape(n_even, 1, LANES),
        lam_even=lam_even, da_norm=da_norm_even.reshape(n_even, 1, LANES),
        ml_norm=ml_norm_even.reshape(n_even, 1, ML_DIM),
        w_out_even=w_out_even.astype(BF16), w_in_odd=w_in_odd.astype(BF16),
        gq_odd=jnp.tile(qk_gain_odd[:, 0], (1, 2)).reshape(n_odd, 1, LANES),
        gk_odd=jnp.tile(qk_gain_odd[:, 1], (1, 2)).reshape(n_odd, 1, LANES),
        w_out_odd=w_out_odd.astype(BF16), w_ffn_in=w_ffn_in.astype(BF16), w_ffn_out=w_ffn_out.astype(BF16))

    cond8 = jnp.concatenate([c_ctx[None, :], c, jnp.zeros((8 - 1 - dbatch, D_MODEL), F32)], axis=0)
    mod = _modulation(cond8, w_mod, b_mod)
    rope_tabs = _rope_tables(dseq)

    x = x_prompt.reshape(batch * seq, D_MODEL)
    ctx_even, ctx_odd = [], []
    for li in range(depth):
        x, new = _trunk(x, mod[li, 0:1].reshape(1, 1, 6 * D_MODEL), li, P, batch, None, None, tm)
        (ctx_even if li % 2 == 0 else ctx_odd).append(new)
    y_prompt = x.reshape(batch, seq, D_MODEL)

    x = x_sample.reshape(dbatch * dseq, D_MODEL)
    for li in range(depth):
        j = li // 2
        if li % 2 == 0:
            ctx = (cache_da_k[:, j].reshape(dbatch * past, SEG).astype(BF16),
                   _with_ones(cache_da_v[:, j].reshape(dbatch * past, SEG)),
                   (state_mlstm_C[:, j], state_mlstm_n[:, j][:, :, :, None, :],
                    jnp.broadcast_to(state_mlstm_m[:, j][:, :, :, None, None], (dbatch, 2, ML_HEADS, 1, ML_DIM))))
        else:
            ctx = (cache_gqa_k[:, j].reshape(dbatch * past, GQA_KV_HEADS * HEAD_DIM).astype(BF16),
                   _with_ones(cache_gqa_v[:, j].reshape(dbatch * past, GQA_KV_HEADS * HEAD_DIM)))
        x, _ = _trunk(x, mod[li, 1:1 + dbatch].reshape(dbatch, 1, 6 * D_MODEL), li, P, dbatch, rope_tabs, ctx, tm)
    y_sample = x.reshape(dbatch, dseq, D_MODEL)

    new_da_k = jnp.stack([t[0].reshape(batch, seq, DA_HEADS, 2, HEAD_DIM) for t in ctx_even], axis=1)
    new_da_v = jnp.stack([t[1].reshape(batch, seq, DA_HEADS, 2 * HEAD_DIM) for t in ctx_even], axis=1)
    new_c = jnp.stack([t[2] for t in ctx_even], axis=1)
    new_n = jnp.stack([t[3][:, :, :, 0, :] for t in ctx_even], axis=1)
    new_m = jnp.stack([t[4][:, :, :, 0, 0] for t in ctx_even], axis=1)
    new_gqa_k = jnp.stack([t[0].reshape(batch, seq, GQA_KV_HEADS, HEAD_DIM) for t in ctx_odd], axis=1)
    new_gqa_v = jnp.stack([t[1].reshape(batch, seq, GQA_KV_HEADS, HEAD_DIM) for t in ctx_odd], axis=1)
    return (y_prompt, y_sample, new_da_k, new_da_v, new_c, new_n, new_m, new_gqa_k, new_gqa_v)
```

```python
import functools
import math

import jax
import jax.numpy as jnp
from jax import lax
from jax.experimental import pallas as pl
from jax.experimental.pallas import tpu as pltpu

F32 = jnp.float32
BF16 = jnp.bfloat16
HIGHEST = lax.Precision.HIGHEST

D_MODEL = 1024
GRID_W = 64
HEAD_DIM = 64
ROPE_PAIRS = HEAD_DIM // 4
ROPE_THETA = 10000.0
EPS = 1e-6
DA_HEADS = 4
ML_HEADS = 4
ML_DIM = 128
N_GATES = 2 * 2 * ML_HEADS
GQA_Q_HEADS = 16
GQA_KV_HEADS = 4
SEG = 512
EVEN_MAIN = 7 * SEG
D_FF = -(-8 * D_MODEL // (3 * 256)) * 256

LANES = 128
VMEM_LIMIT = 56 * 2 ** 20

ATTN_TQ = 256
ATTN_TK = 256
ATTN_LAG = 3
Q_SCALE = math.log2(math.e) * HEAD_DIM ** -0.5
ML_CHUNK = 128
FF_CHUNK = 256


def _params(sem, vmem=VMEM_LIMIT):
    return pltpu.CompilerParams(dimension_semantics=sem, vmem_limit_bytes=vmem)


def _rms_rows(x, g):
    return x * lax.rsqrt(jnp.mean(x * x, axis=-1, keepdims=True) + EPS) * g


def _lane_lo():
    return lax.broadcasted_iota(jnp.int32, (1, LANES), 1) < HEAD_DIM


def _half_rms(x, gain, lo):
    x2 = x * x
    s_lo = jnp.sum(jnp.where(lo, x2, 0.0), axis=-1, keepdims=True)
    s_hi = jnp.sum(jnp.where(lo, 0.0, x2), axis=-1, keepdims=True)
    r = jnp.where(lo, lax.rsqrt(s_lo * (1.0 / HEAD_DIM) + EPS), lax.rsqrt(s_hi * (1.0 / HEAD_DIM) + EPS))
    return x * r * gain


def _rope(y, cos, sin_a, sin_b):
    return y * cos + pltpu.roll(y, LANES - ROPE_PAIRS, 1) * sin_a + pltpu.roll(y, ROPE_PAIRS, 1) * sin_b


def _sigmoid(x):
    return 1.0 / (1.0 + jnp.exp(-x))


def _store_with_ones(v_o, v):
    ones = jnp.ones((v.shape[0], LANES), BF16)
    for g in range(v.shape[1] // LANES):
        v_o[:, 2 * g * LANES:(2 * g + 1) * LANES] = v[:, g * LANES:(g + 1) * LANES].astype(BF16)
        v_o[:, (2 * g + 1) * LANES:(2 * g + 2) * LANES] = ones


def _mod_kernel(cond_ref, w_ref, b_ref, o_ref):
    c = cond_ref[...]
    a = c * _sigmoid(c)
    o_ref[0] = jnp.dot(a, w_ref[0], preferred_element_type=F32, precision=HIGHEST) + b_ref[0]


def _modulation(cond8, w_mod, b_mod):
    depth, _, n = w_mod.shape
    tn = 1024
    return pl.pallas_call(
        _mod_kernel,
        grid=(depth, n // tn),
        in_specs=[pl.BlockSpec((8, D_MODEL), lambda l, j: (0, 0)),
                  pl.BlockSpec((1, D_MODEL, tn), lambda l, j: (l, 0, j)),
                  pl.BlockSpec((1, 1, tn), lambda l, j: (l, 0, j))],
        out_specs=pl.BlockSpec((1, 8, tn), lambda l, j: (l, 0, j)),
        out_shape=jax.ShapeDtypeStruct((depth, 8, n), F32),
        compiler_params=_params(("arbitrary", "arbitrary")), name="modulation",
    )(cond8, w_mod, b_mod.reshape(depth, 1, n))


def _modulated(x_ref, mod_ref, norm_ref, which):
    sh = mod_ref[0, :, (3 * which) * D_MODEL:(3 * which + 1) * D_MODEL]
    sc = mod_ref[0, :, (3 * which + 1) * D_MODEL:(3 * which + 2) * D_MODEL]
    return _rms_rows(x_ref[...], norm_ref[...]) * (1.0 + sc) + sh


def _even_inproj_kernel(*refs, rope, cache):
    it = iter(refs)
    x_ref, mod_ref, n1_ref, w_ref, bg_ref, gq_ref, gk_ref = [next(it) for _ in range(7)]
    if rope:
        cos_ref, sa_ref, sb_ref = [next(it) for _ in range(3)]
    q_o, k_o, v_o, mq_o, mk_o, mv_o, mo_o, gt_o = [next(it) for _ in range(8)]
    if cache:
        kf_o, vf_o = [next(it) for _ in range(2)]

    hb = _modulated(x_ref, mod_ref, n1_ref, 0).astype(BF16)
    lo = _lane_lo()

    def seg(i):
        return jnp.dot(hb, w_ref[:, i * SEG:(i + 1) * SEG], preferred_element_type=F32)

    def qk_norm(p, gain_ref, hh):
        y = _half_rms(p[:, hh * LANES:(hh + 1) * LANES], gain_ref[...], lo)
        if rope:
            y = _rope(y, cos_ref[...], sa_ref[...], sb_ref[...])
        return y

    p = seg(0)
    for hh in range(DA_HEADS):
        q_o[:, hh * LANES:(hh + 1) * LANES] = (qk_norm(p, gq_ref, hh) * Q_SCALE).astype(BF16)
    p = seg(1)
    for hh in range(DA_HEADS):
        sl = slice(hh * LANES, (hh + 1) * LANES)
        if cache:
            y = _half_rms(p[:, sl], gk_ref[...], lo)
            kf_o[:, sl] = y
        else:
            y = qk_norm(p, gk_ref, hh)
        k_o[:, sl] = y.astype(BF16)
    p = seg(2)
    if cache:
        v_o[...] = p.astype(BF16)
        vf_o[...] = p
    else:
        _store_with_ones(v_o, p)
    mq_o[...] = seg(3).astype(BF16)
    mk_o[...] = (seg(4) * (ML_DIM ** -0.5)).astype(BF16)
    mv_o[...] = seg(5).astype(BF16)
    mo_o[...] = _sigmoid(seg(6)).astype(BF16)
    pg = jnp.dot(hb, w_ref[:, EVEN_MAIN:EVEN_MAIN + LANES], preferred_element_type=F32) + bg_ref[...]
    gt_o[...] = pg.T[:N_GATES, :]


def _even_inproj(x, mod, norm1, w_main, b_gate, gain_q, gain_k, rope_tabs, rows_per_batch, cache, tm):
    r = x.shape[0]
    nt = r // tm
    tpb = rows_per_batch // tm
    row = lambda i: (i, 0)
    const = lambda i: (0, 0)
    in_specs = [pl.BlockSpec((tm, D_MODEL), row),
                pl.BlockSpec((1, 1, 6 * D_MODEL), lambda i: (i // tpb, 0, 0)),
                pl.BlockSpec((1, D_MODEL), const),
                pl.BlockSpec((D_MODEL, EVEN_MAIN + LANES), const),
                pl.BlockSpec((1, LANES), const),
                pl.BlockSpec((1, LANES), const),
                pl.BlockSpec((1, LANES), const)]
    args = [x, mod, norm1, w_main, b_gate, gain_q, gain_k]
    if rope_tabs is not None:
        in_specs += [pl.BlockSpec((tm, LANES), lambda i: (i % tpb, 0))] * 3
        args += list(rope_tabs)
    out_shape = [jax.ShapeDtypeStruct((r, SEG), BF16)] * 7
    out_specs = [pl.BlockSpec((tm, SEG), row)] * 7
    if not cache:
        out_shape[2] = jax.ShapeDtypeStruct((r, 2 * SEG), BF16)
        out_specs[2] = pl.BlockSpec((tm, 2 * SEG), row)
    out_shape += [jax.ShapeDtypeStruct((N_GATES, r), F32)]
    out_specs += [pl.BlockSpec((N_GATES, tm), lambda i: (0, i))]
    if cache:
        out_shape += [jax.ShapeDtypeStruct((r, SEG), F32)] * 2
        out_specs += [pl.BlockSpec((tm, SEG), row)] * 2
    return pl.pallas_call(
        functools.partial(_even_inproj_kernel, rope=rope_tabs is not None, cache=cache),
        grid=(nt,), in_specs=in_specs, out_specs=out_specs, out_shape=out_shape,
        compiler_params=_params(("arbitrary",)), name="even_inproj_ctx" if cache else "even_inproj_dec",
    )(*args)


def _odd_inproj_kernel(*refs, rope, cache):
    it = iter(refs)
    x_ref, mod_ref, n1_ref, w_ref, gq_ref, gk_ref = [next(it) for _ in range(6)]
    if rope:
        cos_ref, sa_ref, sb_ref = [next(it) for _ in range(3)]
    q_o, k_o, v_o = [next(it) for _ in range(3)]
    if cache:
        kf_o, vf_o = [next(it) for _ in range(2)]

    hb = _modulated(x_ref, mod_ref, n1_ref, 0).astype(BF16)
    lo = _lane_lo()
    nq = GQA_Q_HEADS * HEAD_DIM
    nkv = GQA_KV_HEADS * HEAD_DIM

    def qk_norm(t, gain_ref):
        y = _half_rms(t, gain_ref[...], lo)
        if rope:
            y = _rope(y, cos_ref[...], sa_ref[...], sb_ref[...])
        return y

    for c in range(nq // SEG):
        p = jnp.dot(hb, w_ref[:, c * SEG:(c + 1) * SEG], preferred_element_type=F32)
        for hh in range(SEG // LANES):
            sl = slice(c * SEG + hh * LANES, c * SEG + (hh + 1) * LANES)
            q_o[:, sl] = (qk_norm(p[:, hh * LANES:(hh + 1) * LANES], gq_ref) * Q_SCALE).astype(BF16)
    p = jnp.dot(hb, w_ref[:, nq:nq + 2 * nkv], preferred_element_type=F32)
    for hh in range(nkv // LANES):
        sl = slice(hh * LANES, (hh + 1) * LANES)
        if cache:
            y = _half_rms(p[:, sl], gk_ref[...], lo)
            kf_o[:, sl] = y
        else:
            y = qk_norm(p[:, sl], gk_ref)
        k_o[:, sl] = y.astype(BF16)
    v = p[:, nkv:2 * nkv]
    if cache:
        v_o[...] = v.astype(BF16)
        vf_o[...] = v
    else:
        _store_with_ones(v_o, v)


def _odd_inproj(x, mod, norm1, w, gain_q, gain_k, rope_tabs, rows_per_batch, cache, tm):
    r = x.shape[0]
    nt = r // tm
    tpb = rows_per_batch // tm
    nq = GQA_Q_HEADS * HEAD_DIM
    nkv = GQA_KV_HEADS * HEAD_DIM
    row = lambda i: (i, 0)
    const = lambda i: (0, 0)
    in_specs = [pl.BlockSpec((tm, D_MODEL), row),
                pl.BlockSpec((1, 1, 6 * D_MODEL), lambda i: (i // tpb, 0, 0)),
                pl.BlockSpec((1, D_MODEL), const),
                pl.BlockSpec((D_MODEL, nq + 2 * nkv), const),
                pl.BlockSpec((1, LANES), const),
                pl.BlockSpec((1, LANES), const)]
    args = [x, mod, norm1, w, gain_q, gain_k]
    if rope_tabs is not None:
        in_specs += [pl.BlockSpec((tm, LANES), lambda i: (i % tpb, 0))] * 3
        args += list(rope_tabs)
    out_shape = [jax.ShapeDtypeStruct((r, nq), BF16), jax.ShapeDtypeStruct((r, nkv), BF16),
                 jax.ShapeDtypeStruct((r, nkv), BF16)]
    out_specs = [pl.BlockSpec((tm, nq), row), pl.BlockSpec((tm, nkv), row), pl.BlockSpec((tm, nkv), row)]
    if not cache:
        out_shape[2] = jax.ShapeDtypeStruct((r, 2 * nkv), BF16)
        out_specs[2] = pl.BlockSpec((tm, 2 * nkv), row)
    if cache:
        out_shape += [jax.ShapeDtypeStruct((r, nkv), F32)] * 2
        out_specs += [pl.BlockSpec((tm, nkv), row)] * 2
    return pl.pallas_call(
        functools.partial(_odd_inproj_kernel, rope=rope_tabs is not None, cache=cache),
        grid=(nt,), in_specs=in_specs, out_specs=out_specs, out_shape=out_shape,
        compiler_params=_params(("arbitrary",)), name="odd_inproj_ctx" if cache else "odd_inproj_dec",
    )(*args)


_NT = (((1,), (1,)), ((), ()))


def _attend(q, k, v):
    s = lax.dot_general(q, k, _NT, preferred_element_type=F32)
    yield
    p = jnp.exp2(s - jnp.max(s, axis=-1, keepdims=True))
    l = jnp.sum(p, axis=-1, keepdims=True)
    yield
    acc = jnp.dot(p.astype(BF16), v, preferred_element_type=F32)
    yield
    return acc / l


def _lam_full(lam_ref, lam_init):
    lam = lam_ref[...]
    return (jnp.exp(jnp.sum(lam[0:1] * lam[1:2], axis=-1, keepdims=True))
            - jnp.exp(jnp.sum(lam[2:3] * lam[3:4], axis=-1, keepdims=True)) + lam_init)


def _da_ctx_kernel(q_ref, k_ref, v_ref, lam_ref, gn_ref, o_ref, *, lam_init):
    lo = _lane_lo()
    lam_full = _lam_full(lam_ref, lam_init)
    chains = []
    for h in range(DA_HEADS):
        sl = slice(h * LANES, (h + 1) * LANES)
        q, k, v = q_ref[:, sl], k_ref[:, sl], v_ref[:, sl]
        zero = jnp.zeros_like(q)
        chains += [_attend(jnp.where(lo, q, zero), k, v), _attend(jnp.where(lo, zero, q), k, v)]
    maps = _round_robin(chains)
    for h in range(DA_HEADS):
        o = maps[2 * h] - lam_full * maps[2 * h + 1]
        o_ref[:, h * LANES:(h + 1) * LANES] = (_rms_rows(o, gn_ref[...]) * (1.0 - lam_init)).astype(BF16)


def _da_ctx_attention(q, k, v, lam, gain, batch, lam_init):
    r = q.shape[0]
    s = r // batch
    blk = pl.BlockSpec((s, SEG), lambda b: (b, 0))
    const = lambda b: (0, 0)
    return pl.pallas_call(
        functools.partial(_da_ctx_kernel, lam_init=lam_init), grid=(batch,),
        in_specs=[blk, blk, blk, pl.BlockSpec(lam.shape, const), pl.BlockSpec((1, LANES), const)],
        out_specs=blk, out_shape=jax.ShapeDtypeStruct((r, SEG), BF16),
        compiler_params=_params(("arbitrary",)), name="da_attn_ctx",
    )(q, k, v, lam, gain)


def _gqa_ctx_kernel(q_ref, k_ref, v_ref, o_ref):
    lo = _lane_lo()
    group_pairs = GQA_Q_HEADS // GQA_KV_HEADS // 2
    chains, keeps = [], []
    for j in range(GQA_KV_HEADS // 2):
        k, v = k_ref[:, j * LANES:(j + 1) * LANES], v_ref[:, j * LANES:(j + 1) * LANES]
        for half in range(2):
            keep = lo if half == 0 else jnp.logical_not(lo)
            for pr in range(group_pairs):
                col = ((j * 2 + half) * group_pairs + pr) * LANES
                qp = q_ref[:, col:col + LANES].astype(F32)
                qr = pltpu.roll(qp, HEAD_DIM, 1)
                chains.append(_attend(jnp.where(keep, jnp.where(lo, qp, qr), 0.0).astype(BF16), k, v))
                chains.append(_attend(jnp.where(keep, jnp.where(lo, qr, qp), 0.0).astype(BF16), k, v))
                keeps.append(keep)
    heads = _round_robin(chains)
    for pair, keep in enumerate(keeps):
        oa, ob = heads[2 * pair], heads[2 * pair + 1]
        first = jnp.where(keep, oa, pltpu.roll(oa, HEAD_DIM, 1))
        second = jnp.where(keep, ob, pltpu.roll(ob, HEAD_DIM, 1))
        o_ref[:, pair * LANES:(pair + 1) * LANES] = jnp.where(lo, first, second).astype(BF16)


def _gqa_ctx_attention(q, k, v, batch):
    r = q.shape[0]
    s = r // batch
    row = lambda b: (b, 0)
    return pl.pallas_call(
        _gqa_ctx_kernel, grid=(batch,),
        in_specs=[pl.BlockSpec((s, q.shape[1]), row), pl.BlockSpec((s, k.shape[1]), row),
                  pl.BlockSpec((s, v.shape[1]), row)],
        out_specs=pl.BlockSpec((s, q.shape[1]), row), out_shape=jax.ShapeDtypeStruct(q.shape, BF16),
        compiler_params=_params(("arbitrary",)), name="gqa_attn_ctx",
    )(q, k, v)


def _key_chunks(ref, cache_ref):
    chunks, col = [], 0
    for r in (ref, cache_ref):
        for c in range(r.shape[0] // ATTN_TK):
            chunks.append((r, c * ATTN_TK, col))
            col += ATTN_TK
    return chunks


def _score_head(q, k_ref, ck_ref, s_w, m_w, a):
    mx = None
    for ref, off, col in _key_chunks(k_ref, ck_ref):
        s = lax.dot_general(q, ref[off:off + ATTN_TK, :], _NT, preferred_element_type=F32)
        s_w[a, :, col:col + ATTN_TK] = s
        tile_max = functools.reduce(jnp.maximum, [s[:, i * LANES:(i + 1) * LANES] for i in range(ATTN_TK // LANES)])
        mx = tile_max if mx is None else jnp.maximum(mx, tile_max)
        yield
    m_w[a] = jnp.max(mx, axis=-1, keepdims=True)


def _finish_head(v_ref, cv_ref, s_r, m_r, a):
    m = m_r[a]
    acc = None
    for ref, off, col in _key_chunks(v_ref, cv_ref):
        p = jnp.exp2(s_r[a, :, col:col + ATTN_TK] - m).astype(BF16)
        part = jnp.dot(p, ref[off:off + ATTN_TK, :], preferred_element_type=F32)
        acc = part if acc is None else acc + part
        yield
    return acc[:, 0:LANES] / acc[:, LANES:LANES + 1]


def _score_and_finish(qs, k_ref, ck_ref, s_w, m_w, v_ref, cv_ref, s_r, m_r):
    def lagged(gen):
        for _ in range(ATTN_LAG):
            yield
        return (yield from gen)

    stages = [_finish_head(v_ref, cv_ref, s_r, m_r, a) for a in range(2)]
    stages += [lagged(_score_head(qs[a], k_ref, ck_ref, s_w, m_w, a)) for a in range(2)]
    results = _round_robin(stages)
    return results[0], results[1]


def _by_parity(t, body, sa, ma, sb, mb):
    @pl.when(t == 0)
    def _():
        sb[...] = jnp.zeros(sb.shape, F32)
        mb[...] = jnp.zeros(mb.shape, F32)

    @pl.when(t % 2 == 0)
    def _():
        body(sa, ma, sb, mb)

    @pl.when(t % 2 == 1)
    def _():
        body(sb, mb, sa, ma)


def _da_pipe_kernel(q_ref, k_ref, ck_ref, v_ref, cv_ref, lam_ref, gn_ref, o_ref, sa, ma, sb, mb, *, lam_init):
    lo = _lane_lo()

    def body(s_w, m_w, s_r, m_r):
        q = q_ref[...]
        zero = jnp.zeros_like(q)
        a1, a2 = _score_and_finish([jnp.where(lo, q, zero), jnp.where(lo, zero, q)], k_ref, ck_ref, s_w, m_w,
                                   v_ref, cv_ref, s_r, m_r)
        o = a1 - _lam_full(lam_ref, lam_init) * a2
        o_ref[...] = (_rms_rows(o, gn_ref[...]) * (1.0 - lam_init)).astype(BF16)

    _by_parity(pl.program_id(0), body, sa, ma, sb, mb)


def _gqa_pipe_kernel(q_ref, k_ref, ck_ref, v_ref, cv_ref, o_ref, sa, ma, sb, mb, *, n_items):
    t = pl.program_id(0)
    lo = _lane_lo()
    q_pairs = GQA_Q_HEADS // GQA_KV_HEADS

    def keep_of(item):
        half = (item % q_pairs) // (q_pairs // 2)
        return jnp.where(lo, 0, 1) == half

    def body(s_w, m_w, s_r, m_r):
        keep = keep_of(jnp.minimum(t, n_items - 1))
        qp = q_ref[...].astype(F32)
        qr = pltpu.roll(qp, HEAD_DIM, 1)
        qa = jnp.where(keep, jnp.where(lo, qp, qr), 0.0).astype(BF16)
        qb = jnp.where(keep, jnp.where(lo, qr, qp), 0.0).astype(BF16)
        oa, ob = _score_and_finish([qa, qb], k_ref, ck_ref, s_w, m_w, v_ref, cv_ref, s_r, m_r)
        keep = keep_of(jnp.maximum(t - 1, 0))
        first = jnp.where(keep, oa, pltpu.roll(oa, HEAD_DIM, 1))
        second = jnp.where(keep, ob, pltpu.roll(ob, HEAD_DIM, 1))
        o_ref[...] = jnp.where(lo, first, second).astype(BF16)

    _by_parity(t, body, sa, ma, sb, mb)


def _pipe_attention(kind, q, k, v1, ck, cv1, batch, tq, extra=(), lam_init=None):
    r = q.shape[0]
    s = r // batch
    past = ck.shape[0] // batch
    nq = s // tq
    groups = k.shape[1] // LANES
    per_q = q.shape[1] // LANES // groups
    per_g = nq * per_q
    n_items = batch * groups * per_g

    def split(item):
        b, g, rest = item // (groups * per_g), (item // per_g) % groups, item % per_g
        return b, g, rest // per_q, rest % per_q

    def q_index(item):
        b, g, i, p = split(item)
        return (b * nq + i, g * per_q + p)

    def kv_index(item):
        b, g, _, _ = split(item)
        return (b, g)

    cur = lambda t: jnp.minimum(t, n_items - 1)
    prev = lambda t: jnp.maximum(t - 1, 0)
    in_specs = [pl.BlockSpec((tq, LANES), lambda t: q_index(cur(t))),
                pl.BlockSpec((s, LANES), lambda t: kv_index(cur(t))),
                pl.BlockSpec((past, LANES), lambda t: kv_index(cur(t))),
                pl.BlockSpec((s, 2 * LANES), lambda t: kv_index(prev(t))),
                pl.BlockSpec((past, 2 * LANES), lambda t: kv_index(prev(t)))]
    in_specs += [pl.BlockSpec(e.shape, lambda t: (0, 0)) for e in extra]
    if kind == "da":
        body = functools.partial(_da_pipe_kernel, lam_init=lam_init)
    else:
        body = functools.partial(_gqa_pipe_kernel, n_items=n_items)
    scratch = [pltpu.VMEM((2, tq, s + past), F32), pltpu.VMEM((2, tq, 1), F32)] * 2
    return pl.pallas_call(
        body, grid=(n_items + 1,), in_specs=in_specs,
        out_specs=pl.BlockSpec((tq, LANES), lambda t: q_index(prev(t))),
        out_shape=jax.ShapeDtypeStruct(q.shape, BF16),
        scratch_shapes=scratch,
        compiler_params=_params(("arbitrary",)),
        name=kind + "_attn_dec",
    )(q, k, ck, v1, cv1, *extra)


def _log_sigmoid(x):
    return jnp.minimum(x, 0.0) - jnp.log1p(jnp.exp(-jnp.abs(x)))


ML_PAD = 8


def _round_robin(generators):
    results = [None] * len(generators)
    live = list(enumerate(generators))
    while live:
        still = []
        for idx, gen in live:
            try:
                next(gen)
                still.append((idx, gen))
            except StopIteration as stop:
                results[idx] = stop.value
        live = still
    return results


def _mlstm_chunk(q, k, v, gt, c_s, n_s, m_s, head, direction):
    L, D = ML_CHUNK, ML_DIM
    i_idx = direction * 2 * ML_HEADS + head
    f_idx = i_idx + ML_HEADS
    i_row, f_row = gt[i_idx:i_idx + 1, :], gt[f_idx:f_idx + 1, :]
    lf_row = _log_sigmoid(f_row)
    r_i = lax.broadcasted_iota(jnp.int32, (L, L), 0)
    c_i = lax.broadcasted_iota(jnp.int32, (L, L), 1)
    vis = (r_i <= c_i) if direction == 0 else (r_i >= c_i)
    between = (c_i > r_i) if direction == 0 else (c_i < r_i)
    lhs_terms, lf_rem, i_rem = [], lf_row, i_row
    for _ in range(3):
        lf_t, i_t = lf_rem.astype(BF16), i_rem.astype(BF16)
        lf_rem, i_rem = lf_rem - lf_t.astype(F32), i_rem - i_t.astype(F32)
        lf_f, i_f = lf_t.astype(F32), i_t.astype(F32)
        lhs_terms.append(jnp.concatenate([
            jnp.concatenate([jnp.where(between, lf_f, 0.0), jnp.where(r_i == c_i, i_f, 0.0)], axis=1),
            jnp.concatenate([jnp.broadcast_to(lf_f, (ML_PAD, L)), jnp.zeros((ML_PAD, L), F32)], axis=1)],
            axis=0).astype(BF16))
    rhs = jnp.concatenate([jnp.where(vis, 1.0, 0.0), jnp.ones((L, L), F32)], axis=0).astype(BF16)
    lhs, rhs = jnp.concatenate(lhs_terms, axis=1), jnp.concatenate([rhs] * 3, axis=0)
    yield
    d_all = jnp.dot(lhs, rhs, preferred_element_type=F32)
    yield
    b_row = d_all[L:L + 1, :]
    last = slice(L - 1, L) if direction == 0 else slice(0, 1)
    b_last = b_row[:, last]
    m_prev = m_s[:, 0:1]
    dmat = jnp.where(vis, d_all[0:L, :], -jnp.inf)
    inter = b_row + m_prev
    m_t = jnp.maximum(inter, jnp.max(dmat, axis=0, keepdims=True))
    w_intra = jnp.exp(dmat - m_t)
    w_inter = jnp.exp(inter - m_t)
    yield
    a_t = (lax.dot_general(k, q, _NT, preferred_element_type=F32) * w_intra).astype(BF16)
    v_t = v.astype(F32).T
    v1 = jnp.concatenate([v_t, jnp.ones((ML_PAD, L), F32)], axis=0).astype(BF16)
    state = jnp.concatenate([c_s[...], jnp.broadcast_to(n_s[...], (ML_PAD, D))], axis=0)
    yield
    tot = (jnp.dot(v1, a_t, preferred_element_type=F32)
           + w_inter * lax.dot_general(state.astype(BF16), q, _NT, preferred_element_type=F32))
    yield
    h_t = tot[0:D, :] / jnp.maximum(jnp.abs(tot[D:D + 1, :]), jnp.exp(-m_t))
    m_new = m_t[:, last]
    w_s = jnp.exp(b_last - b_row + i_row - m_new)
    decay = jnp.exp(b_last + m_prev - m_new)
    vw = jnp.concatenate([v_t * w_s, jnp.broadcast_to(w_s, (ML_PAD, L))], axis=0).astype(BF16)
    yield
    state = decay * state + jnp.dot(vw, k, preferred_element_type=F32)
    return h_t.T, state[0:D, :], state[D:D + 1, :], jnp.broadcast_to(m_new, (1, D))


def _mlstm_kernel(*refs, has_init):
    it = iter(refs)
    sides = [[next(it) for _ in range(4)] for _ in range(2)]
    if has_init:
        c0, n0, m0 = [next(it) for _ in range(3)]
    h_os = [next(it) for _ in range(2)]
    c_o, n_o, m_o = [next(it) for _ in range(3)]
    c_s, n_s, m_s = [next(it) for _ in range(3)]
    chunk = pl.program_id(1)

    @pl.when(chunk == 0)
    def _():
        if has_init:
            c_s[...] = c0[0]
            n_s[...] = n0[0]
            m_s[...] = m0[0]
        else:
            for ref in (c_s, n_s, m_s):
                ref[...] = jnp.zeros(ref.shape, F32)

    chains = []
    for d, (q_ref, k_ref, v_ref, gt_ref) in enumerate(sides):
        gt = gt_ref[...]
        for h in range(ML_HEADS):
            cols = slice(h * ML_DIM, (h + 1) * ML_DIM)
            chains.append(_mlstm_chunk(q_ref[:, cols], k_ref[:, cols], v_ref[:, cols], gt,
                                       c_s.at[d, h], n_s.at[d, h], m_s.at[d, h], h, d))
    results = _round_robin(chains)
    for d, h_o in enumerate(h_os):
        h_o[...] = jnp.concatenate([r[0] for r in results[d * ML_HEADS:(d + 1) * ML_HEADS]], axis=1)
    c_s[...] = jnp.stack([r[1] for r in results]).reshape(c_s.shape)
    n_s[...] = jnp.stack([r[2] for r in results]).reshape(n_s.shape)
    m_s[...] = jnp.stack([r[3] for r in results]).reshape(m_s.shape)

    @pl.when(chunk == pl.num_programs(1) - 1)
    def _():
        c_o[0] = c_s[...]
        n_o[0] = n_s[...]
        m_o[0] = m_s[...]


def _mlstm(q, k, v, gt, init, batch):
    r = q.shape[0]
    L = ML_CHUNK
    nc = r // batch // L
    width = ML_HEADS * ML_DIM
    fwd = lambda b, c: b * nc + c
    bwd = lambda b, c: b * nc + nc - 1 - c

    def side(pos):
        return ([pl.BlockSpec((L, width), lambda b, c: (pos(b, c), 0))] * 3
                + [pl.BlockSpec((N_GATES, L), lambda b, c: (0, pos(b, c)))])

    in_specs = side(fwd) + side(bwd)
    args = [q, k, v, gt] * 2
    state_shapes = [(2, ML_HEADS, ML_DIM, ML_DIM), (2, ML_HEADS, 1, ML_DIM), (2, ML_HEADS, 1, ML_DIM)]
    state_specs = [pl.BlockSpec((1,) + shp, lambda b, c: (b, 0, 0, 0, 0)) for shp in state_shapes]
    if init is not None:
        in_specs += state_specs
        args += list(init)
    out_shape = ([jax.ShapeDtypeStruct((r, width), F32)] * 2
                 + [jax.ShapeDtypeStruct((batch,) + shp, F32) for shp in state_shapes])
    out_specs = [pl.BlockSpec((L, width), lambda b, c: (fwd(b, c), 0)),
                 pl.BlockSpec((L, width), lambda b, c: (bwd(b, c), 0))] + state_specs
    return pl.pallas_call(
        functools.partial(_mlstm_kernel, has_init=init is not None),
        grid=(batch, nc), in_specs=in_specs, out_specs=out_specs, out_shape=out_shape,
        scratch_shapes=[pltpu.VMEM(shp, F32) for shp in state_shapes],
        compiler_params=_params(("arbitrary", "arbitrary")),
        name="mlstm_dec" if init is not None else "mlstm_ctx",
    )(*args)


def _gate(mod_ref, which):
    return mod_ref[0, :, (3 * which + 2) * D_MODEL:(3 * which + 3) * D_MODEL]


def _even_outproj_kernel(da_ref, hf_ref, hb_ref, mo_ref, gn_ref, w_ref, x_ref, mod_ref, o_ref):
    ys = []
    for hh in range(ML_HEADS):
        sl = slice(hh * ML_DIM, (hh + 1) * ML_DIM)
        y = _rms_rows(hf_ref[:, sl] + hb_ref[:, sl], gn_ref[...]) * mo_ref[:, sl].astype(F32)
        ys.append(y.astype(BF16))
    mix = jnp.concatenate([da_ref[...]] + ys, axis=1)
    acc = jnp.dot(mix, w_ref[...], preferred_element_type=F32)
    o_ref[...] = x_ref[...] + _gate(mod_ref, 0) * acc


def _even_outproj(da, hf, hb, mo, gain, w, x, mod, rows_per_batch, tm):
    r = x.shape[0]
    tpb = rows_per_batch // tm
    row = lambda i: (i, 0)
    const = lambda i: (0, 0)
    return pl.pallas_call(
        _even_outproj_kernel, grid=(r // tm,),
        in_specs=[pl.BlockSpec((tm, SEG), row)] * 4 + [
            pl.BlockSpec((1, ML_DIM), const), pl.BlockSpec(w.shape, const),
            pl.BlockSpec((tm, D_MODEL), row), pl.BlockSpec((1, 1, 6 * D_MODEL), lambda i: (i // tpb, 0, 0))],
        out_specs=pl.BlockSpec((tm, D_MODEL), row),
        out_shape=jax.ShapeDtypeStruct(x.shape, F32),
        compiler_params=_params(("arbitrary",)), name="even_outproj",
    )(da, hf, hb, mo, gain, w, x, mod)


def _odd_outproj_kernel(a_ref, w_ref, x_ref, mod_ref, o_ref):
    acc = jnp.dot(a_ref[...], w_ref[...], preferred_element_type=F32)
    o_ref[...] = x_ref[...] + _gate(mod_ref, 0) * acc


def _odd_outproj(a, w, x, mod, rows_per_batch, tm):
    r = x.shape[0]
    tpb = rows_per_batch // tm
    row = lambda i: (i, 0)
    return pl.pallas_call(
        _odd_outproj_kernel, grid=(r // tm,),
        in_specs=[pl.BlockSpec((tm, a.shape[1]), row), pl.BlockSpec(w.shape, lambda i: (0, 0)),
                  pl.BlockSpec((tm, D_MODEL), row), pl.BlockSpec((1, 1, 6 * D_MODEL), lambda i: (i // tpb, 0, 0))],
        out_specs=pl.BlockSpec((tm, D_MODEL), row),
        out_shape=jax.ShapeDtypeStruct(x.shape, F32),
        compiler_params=_params(("arbitrary",)), name="odd_outproj",
    )(a, w, x, mod)


def _ffn_kernel(x_ref, mod_ref, n2_ref, wi_ref, wo_ref, o_ref):
    hb = _modulated(x_ref, mod_ref, n2_ref, 1).astype(BF16)
    acc = jnp.zeros(o_ref.shape, F32)
    for c in range(D_FF // FF_CHUNK):
        sl = slice(c * FF_CHUNK, (c + 1) * FF_CHUNK)
        gate = jnp.dot(hb, wi_ref[:, sl], preferred_element_type=F32)
        up = jnp.dot(hb, wi_ref[:, D_FF + c * FF_CHUNK:D_FF + (c + 1) * FF_CHUNK], preferred_element_type=F32)
        act = (gate * _sigmoid(gate) * up).astype(BF16)
        acc += jnp.dot(act, wo_ref[sl, :], preferred_element_type=F32)
    o_ref[...] = x_ref[...] + _gate(mod_ref, 1) * acc


def _ffn(x, mod, norm2, w_in, w_out, rows_per_batch, tm):
    r = x.shape[0]
    tpb = rows_per_batch // tm
    row = lambda i: (i, 0)
    const = lambda i: (0, 0)
    return pl.pallas_call(
        _ffn_kernel, grid=(r // tm,),
        in_specs=[pl.BlockSpec((tm, D_MODEL), row), pl.BlockSpec((1, 1, 6 * D_MODEL), lambda i: (i // tpb, 0, 0)),
                  pl.BlockSpec((1, D_MODEL), const), pl.BlockSpec(w_in.shape, const), pl.BlockSpec(w_out.shape, const)],
        out_specs=pl.BlockSpec((tm, D_MODEL), row),
        out_shape=jax.ShapeDtypeStruct(x.shape, F32),
        compiler_params=_params(("arbitrary",)), name="ffn",
    )(x, mod, norm2, w_in, w_out)


def _rope_tables(n_tokens):
    t = jnp.arange(n_tokens)
    pos = jnp.stack([(t // GRID_W).astype(F32), (t % GRID_W).astype(F32)], axis=1)
    freqs = ROPE_THETA ** (-jnp.arange(ROPE_PAIRS, dtype=F32) / ROPE_PAIRS)
    ang = pos[:, :, None] * freqs
    cos, sin = jnp.cos(ang), jnp.sin(ang)
    zero = jnp.zeros_like(sin)
    lay = lambda first, second: jnp.tile(jnp.stack([first, second], axis=2).reshape(n_tokens, HEAD_DIM), (1, 2))
    return lay(cos, cos), lay(-sin, zero), lay(zero, sin)


def _with_ones(v):
    rows, width = v.shape
    v3 = v.reshape(rows, width // LANES, LANES).astype(BF16)
    return jnp.concatenate([v3, jnp.ones_like(v3)], axis=-1).reshape(rows, 2 * width)


def _trunk(x, mods, li, P, batch, rope_tabs, ctx, tm):
    lj = li // 2
    rows_per_batch = x.shape[0] // batch
    mod_rpb = rows_per_batch if mods.shape[0] > 1 else x.shape[0]
    is_ctx = ctx is None
    new = None
    if li % 2 == 0:
        lam_init = 0.8 - 0.6 * math.exp(-0.3 * li)
        outs = _even_inproj(x, mods, P['norm1'][li], P['w_even_main'][lj], P['b_gate'][lj],
                            P['gq_even'][lj], P['gk_even'][lj], rope_tabs, mod_rpb, is_ctx, tm)
        q, k, v, mq, mk, mv, mo, gt = outs[:8]
        ck, cv, init = (None, None, None) if is_ctx else ctx
        if is_ctx:
            da = _da_ctx_attention(q, k, v, P['lam_even'][lj], P['da_norm'][lj], batch, lam_init)
        else:
            da = _pipe_attention("da", q, k, v, ck, cv, batch, ATTN_TQ,
                                 extra=(P['lam_even'][lj], P['da_norm'][lj]), lam_init=lam_init)
        hf, hb, c_n, n_n, m_n = _mlstm(mq, mk, mv, gt, init, batch)
        x = _even_outproj(da, hf, hb, mo, P['ml_norm'][lj], P['w_out_even'][lj], x, mods, mod_rpb, tm)
        if is_ctx:
            new = (outs[8], outs[9], c_n, n_n, m_n)
    else:
        outs = _odd_inproj(x, mods, P['norm1'][li], P['w_in_odd'][lj], P['gq_odd'][lj], P['gk_odd'][lj],
                           rope_tabs, mod_rpb, is_ctx, tm)
        q, k, v = outs[:3]
        ck, cv = (None, None) if is_ctx else ctx
        if is_ctx:
            a = _gqa_ctx_attention(q, k, v, batch)
        else:
            a = _pipe_attention("gqa", q, k, v, ck, cv, batch, ATTN_TQ)
        x = _odd_outproj(a, P['w_out_odd'][lj], x, mods, mod_rpb, tm)
        if is_ctx:
            new = (outs[3], outs[4])
    x = _ffn(x, mods, P['norm2'][li], P['w_ffn_in'][li], P['w_ffn_out'][li], mod_rpb, tm)
    return x, new


def kernel(x_prompt, x_sample, c, cache_da_k, cache_da_v, state_mlstm_C, state_mlstm_n, state_mlstm_m, cache_gqa_k, cache_gqa_v, c_ctx, norm1, norm2, w_mod, b_mod, w_in_even, b_gate_even, qk_gain_even, lam_even, da_norm_even, ml_norm_even, w_out_even, w_in_odd, qk_gain_odd, w_out_odd, w_ffn_in, w_ffn_out):
    batch, seq, _ = x_prompt.shape
    dbatch, dseq, _ = x_sample.shape
    depth = norm1.shape[0]
    n_even, n_odd = w_in_even.shape[0], w_in_odd.shape[0]
    past = cache_da_k.shape[2]
    tm = 512

    P = dict(
        norm1=norm1.reshape(depth, 1, D_MODEL), norm2=norm2.reshape(depth, 1, D_MODEL),
        w_even_main=jnp.pad(w_in_even, ((0, 0), (0, 0), (0, LANES - N_GATES))).astype(BF16),
        b_gate=jnp.pad(b_gate_even.reshape(n_even, 1, N_GATES), ((0, 0), (0, 0), (0, LANES - N_GATES))),
        gq_even=qk_gain_even[:, 0].reshape(n_even, 1, LANES), gk_even=qk_gain_even[:, 1].reshape(n_even, 1, LANES),
        lam_even=lam_even, da_norm=da_norm_even.reshape(n_even, 1, LANES),
        ml_norm=ml_norm_even.reshape(n_even, 1, ML_DIM),
        w_out_even=w_out_even.astype(BF16), w_in_odd=w_in_odd.astype(BF16),
        gq_odd=jnp.tile(qk_gain_odd[:, 0], (1, 2)).reshape(n_odd, 1, LANES),
        gk_odd=jnp.tile(qk_gain_odd[:, 1], (1, 2)).reshape(n_odd, 1, LANES),
        w_out_odd=w_out_odd.astype(BF16), w_ffn_in=w_ffn_in.astype(BF16), w_ffn_out=w_ffn_out.astype(BF16))

    cond8 = jnp.concatenate([c_ctx[None, :], c, jnp.zeros((8 - 1 - dbatch, D_MODEL), F32)], axis=0)
    mod = _modulation(cond8, w_mod, b_mod)
    rope_tabs = _rope_tables(dseq)

    x = x_prompt.reshape(batch * seq, D_MODEL)
    ctx_even, ctx_odd = [], []
    for li in range(depth):
        x, new = _trunk(x, mod[li, 0:1].reshape(1, 1, 6 * D_MODEL), li, P, batch, None, None, tm)
        (ctx_even if li % 2 == 0 else ctx_odd).append(new)
    y_prompt = x.reshape(batch, seq, D_MODEL)

    x = x_sample.reshape(dbatch * dseq, D_MODEL)
    for li in range(depth):
        j = li // 2
        if li % 2 == 0:
            ctx = (cache_da_k[:, j].reshape(dbatch * past, SEG).astype(BF16),
                   _with_ones(cache_da_v[:, j].reshape(dbatch * past, SEG)),
                   (state_mlstm_C[:, j], state_mlstm_n[:, j][:, :, :, None, :],
                    jnp.broadcast_to(state_mlstm_m[:, j][:, :, :, None, None], (dbatch, 2, ML_HEADS, 1, ML_DIM))))
        else:
            ctx = (cache_gqa_k[:, j].reshape(dbatch * past, GQA_KV_HEADS * HEAD_DIM).astype(BF16),
                   _with_ones(cache_gqa_v[:, j].reshape(dbatch * past, GQA_KV_HEADS * HEAD_DIM)))
        x, _ = _trunk(x, mod[li, 1:1 + dbatch].reshape(dbatch, 1, 6 * D_MODEL), li, P, dbatch, rope_tabs, ctx, tm)
    y_sample = x.reshape(dbatch, dseq, D_MODEL)

    new_da_k = jnp.stack([t[0].reshape(batch, seq, DA_HEADS, 2, HEAD_DIM) for t in ctx_even], axis=1)
    new_da_v = jnp.stack([t[1].reshape(batch, seq, DA_HEADS, 2 * HEAD_DIM) for t in ctx_even], axis=1)
    new_c = jnp.stack([t[2] for t in ctx_even], axis=1)
    new_n = jnp.stack([t[3][:, :, :, 0, :] for t in ctx_even], axis=1)
    new_m = jnp.stack([t[4][:, :, :, 0, 0] for t in ctx_even], axis=1)
    new_gqa_k = jnp.stack([t[0].reshape(batch, seq, GQA_KV_HEADS, HEAD_DIM) for t in ctx_odd], axis=1)
    new_gqa_v = jnp.stack([t[1].reshape(batch, seq, GQA_KV_HEADS, HEAD_DIM) for t in ctx_odd], axis=1)
    return (y_prompt, y_sample, new_da_k, new_da_v, new_c, new_n, new_m, new_gqa_k, new_gqa_v)
```

```python
import functools
import math

import jax
import jax.numpy as jnp
from jax import lax
from jax.experimental import pallas as pl
from jax.experimental.pallas import tpu as pltpu

F32 = jnp.float32
BF16 = jnp.bfloat16
HIGHEST = lax.Precision.HIGHEST

D_MODEL = 1024
GRID_W = 64
HEAD_DIM = 64
ROPE_PAIRS = HEAD_DIM // 4
ROPE_THETA = 10000.0
EPS = 1e-6
DA_HEADS = 4
ML_HEADS = 4
ML_DIM = 128
N_GATES = 2 * 2 * ML_HEADS
GQA_Q_HEADS = 16
GQA_KV_HEADS = 4
SEG = 512
EVEN_MAIN = 7 * SEG
D_FF = -(-8 * D_MODEL // (3 * 256)) * 256

LANES = 128
VMEM_LIMIT = 56 * 2 ** 20

ATTN_TQ = 256
ATTN_TK = 256
ATTN_LAG = 3
Q_SCALE = math.log2(math.e) * HEAD_DIM ** -0.5
ML_CHUNK = 128
FF_CHUNK = 256
ML_PAD = 8


def _params(sem, vmem=VMEM_LIMIT):
    return pltpu.CompilerParams(dimension_semantics=sem, vmem_limit_bytes=vmem)


def _round_robin(generators):
    results = [None] * len(generators)
    live = list(enumerate(generators))
    while live:
        still = []
        for idx, gen in live:
            try:
                next(gen)
                still.append((idx, gen))
            except StopIteration as stop:
                results[idx] = stop.value
        live = still
    return results


def _rms_rows(x, g):
    return x * lax.rsqrt(jnp.mean(x * x, axis=-1, keepdims=True) + EPS) * g


def _lane_lo():
    return lax.broadcasted_iota(jnp.int32, (1, LANES), 1) < HEAD_DIM


def _half_rms(x, gain, lo):
    x2 = x * x
    s_lo = jnp.sum(jnp.where(lo, x2, 0.0), axis=-1, keepdims=True)
    s_hi = jnp.sum(jnp.where(lo, 0.0, x2), axis=-1, keepdims=True)
    r = jnp.where(lo, lax.rsqrt(s_lo * (1.0 / HEAD_DIM) + EPS), lax.rsqrt(s_hi * (1.0 / HEAD_DIM) + EPS))
    return x * r * gain


def _half_indicator():
    r = lax.broadcasted_iota(jnp.int32, (LANES, LANES), 0) // HEAD_DIM
    c = lax.broadcasted_iota(jnp.int32, (LANES, LANES), 1) // HEAD_DIM
    return jnp.where(r == c, 1.0, 0.0).astype(BF16)


def _half_rms_mxu(x, gain, ind):
    ss = jnp.dot((x * x).astype(BF16), ind, preferred_element_type=F32)
    return x * lax.rsqrt(ss * (1.0 / HEAD_DIM) + EPS) * gain


def _rope(y, cos, sin_a, sin_b):
    return y * cos + pltpu.roll(y, LANES - ROPE_PAIRS, 1) * sin_a + pltpu.roll(y, ROPE_PAIRS, 1) * sin_b


def _sigmoid(x):
    return 1.0 / (1.0 + jnp.exp(-x))


def _store_cache(ref, cols, y, seq):
    for b in range(y.shape[0] // seq):
        ref[b, 0, :, cols] = y[b * seq:(b + 1) * seq]


def _cache_outputs(cache, rows, width, tm):
    seq, layer, n_layers, prev = cache
    shape = jax.ShapeDtypeStruct((rows // seq, n_layers, seq, width), F32)
    spec = pl.BlockSpec((tm // seq, 1, seq, width), lambda i: (i, layer, 0, 0))
    return shape, spec


def _store_with_ones(v_o, v):
    ones = jnp.ones((v.shape[0], LANES), BF16)
    for g in range(v.shape[1] // LANES):
        v_o[:, 2 * g * LANES:(2 * g + 1) * LANES] = v[:, g * LANES:(g + 1) * LANES].astype(BF16)
        v_o[:, (2 * g + 1) * LANES:(2 * g + 2) * LANES] = ones


def _mod_kernel(cond_ref, w_ref, b_ref, o_ref):
    c = cond_ref[...]
    a = c * _sigmoid(c)
    o_ref[0] = jnp.dot(a, w_ref[0], preferred_element_type=F32, precision=HIGHEST) + b_ref[0]


def _modulation(cond8, w_mod, b_mod):
    depth, _, n = w_mod.shape
    tn = 1024
    return pl.pallas_call(
        _mod_kernel,
        grid=(depth, n // tn),
        in_specs=[pl.BlockSpec((8, D_MODEL), lambda l, j: (0, 0)),
                  pl.BlockSpec((1, D_MODEL, tn), lambda l, j: (l, 0, j)),
                  pl.BlockSpec((1, 1, tn), lambda l, j: (l, 0, j))],
        out_specs=pl.BlockSpec((1, 8, tn), lambda l, j: (l, 0, j)),
        out_shape=jax.ShapeDtypeStruct((depth, 8, n), F32),
        compiler_params=_params(("arbitrary", "arbitrary")), name="modulation",
    )(cond8, w_mod, b_mod.reshape(depth, 1, n))


def _modulated(x_ref, mod_ref, norm_ref, which):
    sh = mod_ref[0, :, (3 * which) * D_MODEL:(3 * which + 1) * D_MODEL]
    sc = mod_ref[0, :, (3 * which + 1) * D_MODEL:(3 * which + 2) * D_MODEL]
    return _rms_rows(x_ref[...], norm_ref[...]) * (1.0 + sc) + sh


def _even_inproj_kernel(*refs, rope, cache, n_prev):
    it = iter(refs)
    x_ref, mod_ref, n1_ref, w_ref, bg_ref, gq_ref, gk_ref = [next(it) for _ in range(7)]
    if rope:
        cos_ref, sa_ref, sb_ref = [next(it) for _ in range(3)]
    for _ in range(n_prev):
        next(it)
    q_o, k_o, v_o, mq_o, mk_o, mv_o, mo_o, gt_o = [next(it) for _ in range(8)]
    if cache:
        kf_o, vf_o = [next(it) for _ in range(2)]

    hb = _modulated(x_ref, mod_ref, n1_ref, 0).astype(BF16)
    lo = _lane_lo()

    def seg(i):
        return jnp.dot(hb, w_ref[:, i * SEG:(i + 1) * SEG], preferred_element_type=F32)

    def qk_norm(p, gain_ref, hh):
        y = _half_rms(p[:, hh * LANES:(hh + 1) * LANES], gain_ref[...], lo)
        if rope:
            y = _rope(y, cos_ref[...], sa_ref[...], sb_ref[...])
        return y

    p = seg(0)
    for hh in range(DA_HEADS):
        q_o[:, hh * LANES:(hh + 1) * LANES] = (qk_norm(p, gq_ref, hh) * Q_SCALE).astype(BF16)
    p = seg(1)
    for hh in range(DA_HEADS):
        sl = slice(hh * LANES, (hh + 1) * LANES)
        y = qk_norm(p, gk_ref, hh)
        if cache:
            _store_cache(kf_o, sl, y, cache)
        k_o[:, sl] = y.astype(BF16)
    p = seg(2)
    if cache:
        v_o[...] = p.astype(BF16)
        _store_cache(vf_o, slice(None), p, cache)
    else:
        _store_with_ones(v_o, p)
    mq_o[...] = seg(3).astype(BF16)
    mk_o[...] = (seg(4) * (ML_DIM ** -0.5)).astype(BF16)
    mv_o[...] = seg(5).astype(BF16)
    mo_o[...] = _sigmoid(seg(6)).astype(BF16)
    pg = jnp.dot(hb, w_ref[:, EVEN_MAIN:EVEN_MAIN + LANES], preferred_element_type=F32) + bg_ref[...]
    gt_o[...] = pg.T[:N_GATES, :]


def _even_inproj(x, mod, norm1, w_main, b_gate, gain_q, gain_k, rope_tabs, rows_per_batch, cache, tm):
    r = x.shape[0]
    nt = r // tm
    tpb = rows_per_batch // tm
    row = lambda i: (i, 0)
    const = lambda i: (0, 0)
    in_specs = [pl.BlockSpec((tm, D_MODEL), row),
                pl.BlockSpec((1, 1, 6 * D_MODEL), lambda i: (i // tpb, 0, 0)),
                pl.BlockSpec((1, D_MODEL), const),
                pl.BlockSpec((D_MODEL, EVEN_MAIN + LANES), const),
                pl.BlockSpec((1, LANES), const),
                pl.BlockSpec((1, LANES), const),
                pl.BlockSpec((1, LANES), const)]
    args = [x, mod, norm1, w_main, b_gate, gain_q, gain_k]
    if rope_tabs is not None:
        in_specs += [pl.BlockSpec((tm, LANES), lambda i: (i % tpb, 0))] * 3
        args += list(rope_tabs)
    out_shape = [jax.ShapeDtypeStruct((r, SEG), BF16)] * 7
    out_specs = [pl.BlockSpec((tm, SEG), row)] * 7
    if not cache:
        out_shape[2] = jax.ShapeDtypeStruct((r, 2 * SEG), BF16)
        out_specs[2] = pl.BlockSpec((tm, 2 * SEG), row)
    out_shape += [jax.ShapeDtypeStruct((N_GATES, r), F32)]
    out_specs += [pl.BlockSpec((N_GATES, tm), lambda i: (0, i))]
    aliases = {}
    if cache:
        shape, spec = _cache_outputs(cache, r, SEG, tm)
        for prev in cache[3] or ():
            aliases[len(args)] = len(out_shape)
            in_specs.append(pl.BlockSpec(memory_space=pl.ANY))
            args.append(prev)
            out_shape.append(shape)
            out_specs.append(spec)
        if not cache[3]:
            out_shape += [shape] * 2
            out_specs += [spec] * 2
    return pl.pallas_call(
        functools.partial(_even_inproj_kernel, rope=rope_tabs is not None, cache=cache[0] if cache else None,
                          n_prev=len(aliases)),
        grid=(nt,), in_specs=in_specs, out_specs=out_specs, out_shape=out_shape, input_output_aliases=aliases,
        compiler_params=_params(("arbitrary",)), name="even_inproj_ctx" if cache else "even_inproj_dec",
    )(*args)


def _odd_inproj_kernel(*refs, rope, cache, n_prev):
    it = iter(refs)
    x_ref, mod_ref, n1_ref, w_ref, gq_ref, gk_ref = [next(it) for _ in range(6)]
    if rope:
        cos_ref, sa_ref, sb_ref = [next(it) for _ in range(3)]
    for _ in range(n_prev):
        next(it)
    q_o, k_o, v_o = [next(it) for _ in range(3)]
    if cache:
        kf_o, vf_o = [next(it) for _ in range(2)]

    hb = _modulated(x_ref, mod_ref, n1_ref, 0).astype(BF16)
    ind = _half_indicator() if rope else None
    lo = _lane_lo()
    nq = GQA_Q_HEADS * HEAD_DIM
    nkv = GQA_KV_HEADS * HEAD_DIM

    def qk_norm(t, gain_ref):
        y = _half_rms_mxu(t, gain_ref[...], ind) if rope else _half_rms(t, gain_ref[...], lo)
        if rope:
            y = _rope(y, cos_ref[...], sa_ref[...], sb_ref[...])
        return y

    for c in range(nq // SEG):
        p = jnp.dot(hb, w_ref[:, c * SEG:(c + 1) * SEG], preferred_element_type=F32)
        for hh in range(SEG // LANES):
            sl = slice(c * SEG + hh * LANES, c * SEG + (hh + 1) * LANES)
            q_o[:, sl] = (qk_norm(p[:, hh * LANES:(hh + 1) * LANES], gq_ref) * Q_SCALE).astype(BF16)
    p = jnp.dot(hb, w_ref[:, nq:nq + 2 * nkv], preferred_element_type=F32)
    for hh in range(nkv // LANES):
        sl = slice(hh * LANES, (hh + 1) * LANES)
        y = qk_norm(p[:, sl], gk_ref)
        if cache:
            _store_cache(kf_o, sl, y, cache)
        k_o[:, sl] = y.astype(BF16)
    v = p[:, nkv:2 * nkv]
    if cache:
        v_o[...] = v.astype(BF16)
        _store_cache(vf_o, slice(None), v, cache)
    else:
        _store_with_ones(v_o, v)


def _odd_inproj(x, mod, norm1, w, gain_q, gain_k, rope_tabs, rows_per_batch, cache, tm):
    r = x.shape[0]
    nt = r // tm
    tpb = rows_per_batch // tm
    nq = GQA_Q_HEADS * HEAD_DIM
    nkv = GQA_KV_HEADS * HEAD_DIM
    row = lambda i: (i, 0)
    const = lambda i: (0, 0)
    in_specs = [pl.BlockSpec((tm, D_MODEL), row),
                pl.BlockSpec((1, 1, 6 * D_MODEL), lambda i: (i // tpb, 0, 0)),
                pl.BlockSpec((1, D_MODEL), const),
                pl.BlockSpec((D_MODEL, nq + 2 * nkv), const),
                pl.BlockSpec((1, LANES), const),
                pl.BlockSpec((1, LANES), const)]
    args = [x, mod, norm1, w, gain_q, gain_k]
    if rope_tabs is not None:
        in_specs += [pl.BlockSpec((tm, LANES), lambda i: (i % tpb, 0))] * 3
        args += list(rope_tabs)
    out_shape = [jax.ShapeDtypeStruct((r, nq), BF16), jax.ShapeDtypeStruct((r, nkv), BF16),
                 jax.ShapeDtypeStruct((r, nkv), BF16)]
    out_specs = [pl.BlockSpec((tm, nq), row), pl.BlockSpec((tm, nkv), row), pl.BlockSpec((tm, nkv), row)]
    if not cache:
        out_shape[2] = jax.ShapeDtypeStruct((r, 2 * nkv), BF16)
        out_specs[2] = pl.BlockSpec((tm, 2 * nkv), row)
    aliases = {}
    if cache:
        shape, spec = _cache_outputs(cache, r, nkv, tm)
        for prev in cache[3] or ():
            aliases[len(args)] = len(out_shape)
            in_specs.append(pl.BlockSpec(memory_space=pl.ANY))
            args.append(prev)
            out_shape.append(shape)
            out_specs.append(spec)
        if not cache[3]:
            out_shape += [shape] * 2
            out_specs += [spec] * 2
    return pl.pallas_call(
        functools.partial(_odd_inproj_kernel, rope=rope_tabs is not None, cache=cache[0] if cache else None,
                          n_prev=len(aliases)),
        grid=(nt,), in_specs=in_specs, out_specs=out_specs, out_shape=out_shape, input_output_aliases=aliases,
        compiler_params=_params(("arbitrary",)), name="odd_inproj_ctx" if cache else "odd_inproj_dec",
    )(*args)


_NT = (((1,), (1,)), ((), ()))


def _attend(q, k, v):
    s = lax.dot_general(q, k, _NT, preferred_element_type=F32)
    yield
    p = jnp.exp2(s - jnp.max(s, axis=-1, keepdims=True))
    l = jnp.sum(p, axis=-1, keepdims=True)
    yield
    acc = jnp.dot(p.astype(BF16), v, preferred_element_type=F32)
    yield
    return acc / l


def _lam_full(lam_ref, lam_init):
    lam = lam_ref[...]
    return (jnp.exp(jnp.sum(lam[0:1] * lam[1:2], axis=-1, keepdims=True))
            - jnp.exp(jnp.sum(lam[2:3] * lam[3:4], axis=-1, keepdims=True)) + lam_init)


def _da_ctx_kernel(q_ref, k_ref, v_ref, lam_ref, gn_ref, o_ref, *, lam_init):
    lo = _lane_lo()
    lam_full = _lam_full(lam_ref, lam_init)
    chains = []
    for h in range(DA_HEADS):
        sl = slice(h * LANES, (h + 1) * LANES)
        q, k, v = q_ref[:, sl], k_ref[:, sl], v_ref[:, sl]
        zero = jnp.zeros_like(q)
        chains += [_attend(jnp.where(lo, q, zero), k, v), _attend(jnp.where(lo, zero, q), k, v)]
    maps = _round_robin(chains)
    for h in range(DA_HEADS):
        o = maps[2 * h] - lam_full * maps[2 * h + 1]
        o_ref[:, h * LANES:(h + 1) * LANES] = (_rms_rows(o, gn_ref[...]) * (1.0 - lam_init)).astype(BF16)


def _da_ctx_attention(q, k, v, lam, gain, batch, lam_init):
    r = q.shape[0]
    s = r // batch
    blk = pl.BlockSpec((s, SEG), lambda b: (b, 0))
    const = lambda b: (0, 0)
    return pl.pallas_call(
        functools.partial(_da_ctx_kernel, lam_init=lam_init), grid=(batch,),
        in_specs=[blk, blk, blk, pl.BlockSpec(lam.shape, const), pl.BlockSpec((1, LANES), const)],
        out_specs=blk, out_shape=jax.ShapeDtypeStruct((r, SEG), BF16),
        compiler_params=_params(("arbitrary",)), name="da_attn_ctx",
    )(q, k, v, lam, gain)


def _gqa_ctx_kernel(q_ref, k_ref, v_ref, o_ref):
    lo = _lane_lo()
    group_pairs = GQA_Q_HEADS // GQA_KV_HEADS // 2
    chains, keeps = [], []
    for j in range(GQA_KV_HEADS // 2):
        k, v = k_ref[:, j * LANES:(j + 1) * LANES], v_ref[:, j * LANES:(j + 1) * LANES]
        for half in range(2):
            keep = lo if half == 0 else jnp.logical_not(lo)
            for pr in range(group_pairs):
                col = ((j * 2 + half) * group_pairs + pr) * LANES
                qp = q_ref[:, col:col + LANES].astype(F32)
                qr = pltpu.roll(qp, HEAD_DIM, 1)
                chains.append(_attend(jnp.where(keep, jnp.where(lo, qp, qr), 0.0).astype(BF16), k, v))
                chains.append(_attend(jnp.where(keep, jnp.where(lo, qr, qp), 0.0).astype(BF16), k, v))
                keeps.append(keep)
    heads = _round_robin(chains)
    for pair, keep in enumerate(keeps):
        oa, ob = heads[2 * pair], heads[2 * pair + 1]
        first = jnp.where(keep, oa, pltpu.roll(oa, HEAD_DIM, 1))
        second = jnp.where(keep, ob, pltpu.roll(ob, HEAD_DIM, 1))
        o_ref[:, pair * LANES:(pair + 1) * LANES] = jnp.where(lo, first, second).astype(BF16)


def _gqa_ctx_attention(q, k, v, batch):
    r = q.shape[0]
    s = r // batch
    row = lambda b: (b, 0)
    return pl.pallas_call(
        _gqa_ctx_kernel, grid=(batch,),
        in_specs=[pl.BlockSpec((s, q.shape[1]), row), pl.BlockSpec((s, k.shape[1]), row),
                  pl.BlockSpec((s, v.shape[1]), row)],
        out_specs=pl.BlockSpec((s, q.shape[1]), row), out_shape=jax.ShapeDtypeStruct(q.shape, BF16),
        compiler_params=_params(("arbitrary",)), name="gqa_attn_ctx",
    )(q, k, v)


def _key_chunks(ref, cache_ref):
    chunks, col = [], 0
    for r in (ref, cache_ref):
        for c in range(r.shape[0] // ATTN_TK):
            chunks.append((r, c * ATTN_TK, col))
            col += ATTN_TK
    return chunks


def _score_head(q, k_ref, ck_ref, s_w, m_w, a):
    mx = None
    for ref, off, col in _key_chunks(k_ref, ck_ref):
        s = lax.dot_general(q, ref[off:off + ATTN_TK, :], _NT, preferred_element_type=F32)
        s_w[a, :, col:col + ATTN_TK] = s
        tile_max = functools.reduce(jnp.maximum, [s[:, i * LANES:(i + 1) * LANES] for i in range(ATTN_TK // LANES)])
        mx = tile_max if mx is None else jnp.maximum(mx, tile_max)
        yield
    m_w[a] = jnp.max(mx, axis=-1, keepdims=True)


def _finish_head(v_ref, cv_ref, s_r, m_r, a):
    m = m_r[a]
    acc = None
    for ref, off, col in _key_chunks(v_ref, cv_ref):
        p = jnp.exp2(s_r[a, :, col:col + ATTN_TK] - m).astype(BF16)
        part = jnp.dot(p, ref[off:off + ATTN_TK, :], preferred_element_type=F32)
        acc = part if acc is None else acc + part
        yield
    return acc[:, 0:LANES] / acc[:, LANES:LANES + 1]


def _score_and_finish(qs, k_ref, ck_ref, s_w, m_w, v_ref, cv_ref, s_r, m_r):
    def lagged(gen):
        for _ in range(ATTN_LAG):
            yield
        return (yield from gen)

    stages = [_finish_head(v_ref, cv_ref, s_r, m_r, a) for a in range(2)]
    stages += [lagged(_score_head(qs[a], k_ref, ck_ref, s_w, m_w, a)) for a in range(2)]
    results = _round_robin(stages)
    return results[0], results[1]


def _by_parity(t, body, sa, ma, sb, mb):
    @pl.when(t == 0)
    def _():
        sb[...] = jnp.zeros(sb.shape, F32)
        mb[...] = jnp.zeros(mb.shape, F32)

    @pl.when(t % 2 == 0)
    def _():
        body(sa, ma, sb, mb)

    @pl.when(t % 2 == 1)
    def _():
        body(sb, mb, sa, ma)


def _da_pipe_kernel(q_ref, k_ref, ck_ref, v_ref, cv_ref, lam_ref, gn_ref, o_ref, sa, ma, sb, mb, *, lam_init):
    lo = _lane_lo()

    def body(s_w, m_w, s_r, m_r):
        q = q_ref[...]
        zero = jnp.zeros_like(q)
        a1, a2 = _score_and_finish([jnp.where(lo, q, zero), jnp.where(lo, zero, q)], k_ref, ck_ref, s_w, m_w,
                                   v_ref, cv_ref, s_r, m_r)
        o = a1 - _lam_full(lam_ref, lam_init) * a2
        o_ref[...] = (_rms_rows(o, gn_ref[...]) * (1.0 - lam_init)).astype(BF16)

    _by_parity(pl.program_id(0), body, sa, ma, sb, mb)


def _gqa_pipe_kernel(q_ref, k_ref, ck_ref, v_ref, cv_ref, o_ref, sa, ma, sb, mb, *, n_items):
    t = pl.program_id(0)
    lo = _lane_lo()
    q_pairs = GQA_Q_HEADS // GQA_KV_HEADS

    def keep_of(item):
        half = (item % q_pairs) // (q_pairs // 2)
        return jnp.where(lo, 0, 1) == half

    def body(s_w, m_w, s_r, m_r):
        keep = keep_of(jnp.minimum(t, n_items - 1))
        qp = q_ref[...].astype(F32)
        qr = pltpu.roll(qp, HEAD_DIM, 1)
        qa = jnp.where(keep, jnp.where(lo, qp, qr), 0.0).astype(BF16)
        qb = jnp.where(keep, jnp.where(lo, qr, qp), 0.0).astype(BF16)
        oa, ob = _score_and_finish([qa, qb], k_ref, ck_ref, s_w, m_w, v_ref, cv_ref, s_r, m_r)
        keep = keep_of(jnp.maximum(t - 1, 0))
        first = jnp.where(keep, oa, pltpu.roll(oa, HEAD_DIM, 1))
        second = jnp.where(keep, ob, pltpu.roll(ob, HEAD_DIM, 1))
        o_ref[...] = jnp.where(lo, first, second).astype(BF16)

    _by_parity(t, body, sa, ma, sb, mb)


def _pipe_attention(kind, q, k, v1, ck, cv1, batch, tq, extra=(), lam_init=None):
    r = q.shape[0]
    s = r // batch
    past = ck.shape[0] // batch
    nq = s // tq
    groups = k.shape[1] // LANES
    per_q = q.shape[1] // LANES // groups
    per_g = nq * per_q
    n_items = batch * groups * per_g

    def split(item):
        b, g, rest = item // (groups * per_g), (item // per_g) % groups, item % per_g
        return b, g, rest // per_q, rest % per_q

    def q_index(item):
        b, g, i, p = split(item)
        return (b * nq + i, g * per_q + p)

    def kv_index(item):
        b, g, _, _ = split(item)
        return (b, g)

    cur = lambda t: jnp.minimum(t, n_items - 1)
    prev = lambda t: jnp.maximum(t - 1, 0)
    in_specs = [pl.BlockSpec((tq, LANES), lambda t: q_index(cur(t))),
                pl.BlockSpec((s, LANES), lambda t: kv_index(cur(t))),
                pl.BlockSpec((past, LANES), lambda t: kv_index(cur(t))),
                pl.BlockSpec((s, 2 * LANES), lambda t: kv_index(prev(t))),
                pl.BlockSpec((past, 2 * LANES), lambda t: kv_index(prev(t)))]
    in_specs += [pl.BlockSpec(e.shape, lambda t: (0, 0)) for e in extra]
    if kind == "da":
        body = functools.partial(_da_pipe_kernel, lam_init=lam_init)
    else:
        body = functools.partial(_gqa_pipe_kernel, n_items=n_items)
    scratch = [pltpu.VMEM((2, tq, s + past), F32), pltpu.VMEM((2, tq, 1), F32)] * 2
    return pl.pallas_call(
        body, grid=(n_items + 1,), in_specs=in_specs,
        out_specs=pl.BlockSpec((tq, LANES), lambda t: q_index(prev(t))),
        out_shape=jax.ShapeDtypeStruct(q.shape, BF16),
        scratch_shapes=scratch,
        compiler_params=_params(("arbitrary",)),
        name=kind + "_attn_dec",
    )(q, k, ck, v1, cv1, *extra)


def _log_sigmoid(x):
    return jnp.minimum(x, 0.0) - jnp.log1p(jnp.exp(-jnp.abs(x)))


def _mlstm_chunk(q, k, v, gt, c_s, n_s, m_s, head, direction):
    L, D = ML_CHUNK, ML_DIM
    i_idx = direction * 2 * ML_HEADS + head
    f_idx = i_idx + ML_HEADS
    i_row, f_row = gt[i_idx:i_idx + 1, :], gt[f_idx:f_idx + 1, :]
    lf_row = _log_sigmoid(f_row)
    r_i = lax.broadcasted_iota(jnp.int32, (L, L), 0)
    c_i = lax.broadcasted_iota(jnp.int32, (L, L), 1)
    vis = (r_i <= c_i) if direction == 0 else (r_i >= c_i)
    between = (c_i > r_i) if direction == 0 else (c_i < r_i)
    lhs_terms, lf_rem, i_rem = [], lf_row, i_row
    for _ in range(3):
        lf_t, i_t = lf_rem.astype(BF16), i_rem.astype(BF16)
        lf_rem, i_rem = lf_rem - lf_t.astype(F32), i_rem - i_t.astype(F32)
        lf_f, i_f = lf_t.astype(F32), i_t.astype(F32)
        lhs_terms.append(jnp.concatenate([
            jnp.concatenate([jnp.where(between, lf_f, 0.0), jnp.where(r_i == c_i, i_f, 0.0)], axis=1),
            jnp.concatenate([jnp.broadcast_to(lf_f, (ML_PAD, L)), jnp.zeros((ML_PAD, L), F32)], axis=1)],
            axis=0).astype(BF16))
    rhs = jnp.concatenate([jnp.where(vis, 1.0, 0.0), jnp.ones((L, L), F32)], axis=0).astype(BF16)
    lhs, rhs = jnp.concatenate(lhs_terms, axis=1), jnp.concatenate([rhs] * 3, axis=0)
    yield
    d_all = jnp.dot(lhs, rhs, preferred_element_type=F32)
    yield
    b_row = d_all[L:L + 1, :]
    last = slice(L - 1, L) if direction == 0 else slice(0, 1)
    b_last = b_row[:, last]
    m_prev = m_s[:, 0:1]
    dmat = jnp.where(vis, d_all[0:L, :], -jnp.inf)
    inter = b_row + m_prev
    m_t = jnp.maximum(inter, jnp.max(dmat, axis=0, keepdims=True))
    w_intra = jnp.exp(dmat - m_t)
    w_inter = jnp.exp(inter - m_t)
    yield
    a_t = (lax.dot_general(k, q, _NT, preferred_element_type=F32) * w_intra).astype(BF16)
    v_t = v.astype(F32).T
    v1 = jnp.concatenate([v_t, jnp.ones((ML_PAD, L), F32)], axis=0).astype(BF16)
    state = jnp.concatenate([c_s[...], jnp.broadcast_to(n_s[...], (ML_PAD, D))], axis=0)
    yield
    tot = (jnp.dot(v1, a_t, preferred_element_type=F32)
           + w_inter * lax.dot_general(state.astype(BF16), q, _NT, preferred_element_type=F32))
    yield
    h_t = tot[0:D, :] / jnp.maximum(jnp.abs(tot[D:D + 1, :]), jnp.exp(-m_t))
    m_new = m_t[:, last]
    w_s = jnp.exp(b_last - b_row + i_row - m_new)
    decay = jnp.exp(b_last + m_prev - m_new)
    vw = jnp.concatenate([v_t * w_s, jnp.broadcast_to(w_s, (ML_PAD, L))], axis=0).astype(BF16)
    yield
    state = decay * state + jnp.dot(vw, k, preferred_element_type=F32)
    return h_t.T, state[0:D, :], state[D:D + 1, :], jnp.broadcast_to(m_new, (1, D))


def _mlstm_kernel(*refs, has_init, n_prev):
    it = iter(refs)
    sides = [[next(it) for _ in range(4)] for _ in range(2)]
    if has_init:
        c0, n0, m0 = [next(it) for _ in range(3)]
    for _ in range(n_prev):
        next(it)
    h_os = [next(it) for _ in range(2)]
    c_o, n_o, m_o = [next(it) for _ in range(3)]
    c_s, n_s, m_s = [next(it) for _ in range(3)]
    chunk = pl.program_id(1)

    @pl.when(chunk == 0)
    def _():
        if has_init:
            c_s[...] = c0[0]
            n_s[...] = n0[0]
            m_s[...] = m0[0]
        else:
            for ref in (c_s, n_s, m_s):
                ref[...] = jnp.zeros(ref.shape, F32)

    chains = []
    for d, (q_ref, k_ref, v_ref, gt_ref) in enumerate(sides):
        gt = gt_ref[...]
        for h in range(ML_HEADS):
            cols = slice(h * ML_DIM, (h + 1) * ML_DIM)
            chains.append(_mlstm_chunk(q_ref[:, cols], k_ref[:, cols], v_ref[:, cols], gt,
                                       c_s.at[d, h], n_s.at[d, h], m_s.at[d, h], h, d))
    results = _round_robin(chains)
    for d, h_o in enumerate(h_os):
        h_o[...] = jnp.concatenate([r[0] for r in results[d * ML_HEADS:(d + 1) * ML_HEADS]], axis=1)
    c_s[...] = jnp.stack([r[1] for r in results]).reshape(c_s.shape)
    n_s[...] = jnp.stack([r[2] for r in results]).reshape(n_s.shape)
    m_s[...] = jnp.stack([r[3] for r in results]).reshape(m_s.shape)

    @pl.when(chunk == pl.num_programs(1) - 1)
    def _():
        c_o[0, 0] = c_s[...]
        n_o[0, 0] = n_s[...]
        m_o[0, 0] = m_s[...]


def _mlstm(q, k, v, gt, init, batch, layer=0, n_layers=1, prev=None):
    r = q.shape[0]
    L = ML_CHUNK
    nc = r // batch // L
    width = ML_HEADS * ML_DIM
    fwd = lambda b, c: b * nc + c
    bwd = lambda b, c: b * nc + nc - 1 - c

    def side(pos):
        return ([pl.BlockSpec((L, width), lambda b, c: (pos(b, c), 0))] * 3
                + [pl.BlockSpec((N_GATES, L), lambda b, c: (0, pos(b, c)))])

    in_specs = side(fwd) + side(bwd)
    args = [q, k, v, gt] * 2
    state_shapes = [(2, ML_HEADS, ML_DIM, ML_DIM), (2, ML_HEADS, 1, ML_DIM), (2, ML_HEADS, 1, ML_DIM)]
    state_specs = [pl.BlockSpec((1,) + shp, lambda b, c: (b, 0, 0, 0, 0)) for shp in state_shapes]
    if init is not None:
        in_specs += state_specs
        args += list(init)
    aliases = {}
    for j, arr in enumerate(prev or ()):
        aliases[len(args)] = 2 + j
        in_specs.append(pl.BlockSpec(memory_space=pl.ANY))
        args.append(arr)
    out_shape = ([jax.ShapeDtypeStruct((r, width), F32)] * 2
                 + [jax.ShapeDtypeStruct((batch, n_layers) + shp, F32) for shp in state_shapes])
    out_specs = [pl.BlockSpec((L, width), lambda b, c: (fwd(b, c), 0)),
                 pl.BlockSpec((L, width), lambda b, c: (bwd(b, c), 0))]
    out_specs += [pl.BlockSpec((1, 1) + shp, lambda b, c: (b, layer, 0, 0, 0, 0)) for shp in state_shapes]
    return pl.pallas_call(
        functools.partial(_mlstm_kernel, has_init=init is not None, n_prev=len(aliases)),
        grid=(batch, nc), in_specs=in_specs, out_specs=out_specs, out_shape=out_shape, input_output_aliases=aliases,
        scratch_shapes=[pltpu.VMEM(shp, F32) for shp in state_shapes],
        compiler_params=_params(("arbitrary", "arbitrary")),
        name="mlstm_dec" if init is not None else "mlstm_ctx",
    )(*args)


def _gate(mod_ref, which):
    return mod_ref[0, :, (3 * which + 2) * D_MODEL:(3 * which + 3) * D_MODEL]


def _even_outproj_kernel(da_ref, hf_ref, hb_ref, mo_ref, gn_ref, w_ref, x_ref, mod_ref, o_ref):
    ys = []
    for hh in range(ML_HEADS):
        sl = slice(hh * ML_DIM, (hh + 1) * ML_DIM)
        y = _rms_rows(hf_ref[:, sl] + hb_ref[:, sl], gn_ref[...]) * mo_ref[:, sl].astype(F32)
        ys.append(y.astype(BF16))
    mix = jnp.concatenate([da_ref[...]] + ys, axis=1)
    acc = jnp.dot(mix, w_ref[...], preferred_element_type=F32)
    o_ref[...] = x_ref[...] + _gate(mod_ref, 0) * acc


def _even_outproj(da, hf, hb, mo, gain, w, x, mod, rows_per_batch, tm):
    r = x.shape[0]
    tpb = rows_per_batch // tm
    row = lambda i: (i, 0)
    const = lambda i: (0, 0)
    return pl.pallas_call(
        _even_outproj_kernel, grid=(r // tm,),
        in_specs=[pl.BlockSpec((tm, SEG), row)] * 4 + [
            pl.BlockSpec((1, ML_DIM), const), pl.BlockSpec(w.shape, const),
            pl.BlockSpec((tm, D_MODEL), row), pl.BlockSpec((1, 1, 6 * D_MODEL), lambda i: (i // tpb, 0, 0))],
        out_specs=pl.BlockSpec((tm, D_MODEL), row),
        out_shape=jax.ShapeDtypeStruct(x.shape, F32),
        compiler_params=_params(("arbitrary",)), name="even_outproj",
    )(da, hf, hb, mo, gain, w, x, mod)


def _odd_outproj_kernel(a_ref, w_ref, x_ref, mod_ref, o_ref):
    acc = jnp.dot(a_ref[...], w_ref[...], preferred_element_type=F32)
    o_ref[...] = x_ref[...] + _gate(mod_ref, 0) * acc


def _odd_outproj(a, w, x, mod, rows_per_batch, tm):
    r = x.shape[0]
    tpb = rows_per_batch // tm
    row = lambda i: (i, 0)
    return pl.pallas_call(
        _odd_outproj_kernel, grid=(r // tm,),
        in_specs=[pl.BlockSpec((tm, a.shape[1]), row), pl.BlockSpec(w.shape, lambda i: (0, 0)),
                  pl.BlockSpec((tm, D_MODEL), row), pl.BlockSpec((1, 1, 6 * D_MODEL), lambda i: (i // tpb, 0, 0))],
        out_specs=pl.BlockSpec((tm, D_MODEL), row),
        out_shape=jax.ShapeDtypeStruct(x.shape, F32),
        compiler_params=_params(("arbitrary",)), name="odd_outproj",
    )(a, w, x, mod)


def _ffn_kernel(x_ref, mod_ref, n2_ref, wi_ref, wo_ref, o_ref):
    hb = _modulated(x_ref, mod_ref, n2_ref, 1).astype(BF16)
    acc = jnp.zeros(o_ref.shape, F32)
    for c in range(D_FF // FF_CHUNK):
        sl = slice(c * FF_CHUNK, (c + 1) * FF_CHUNK)
        gate = jnp.dot(hb, wi_ref[:, sl], preferred_element_type=F32)
        up = jnp.dot(hb, wi_ref[:, D_FF + c * FF_CHUNK:D_FF + (c + 1) * FF_CHUNK], preferred_element_type=F32)
        act = (gate * _sigmoid(gate) * up).astype(BF16)
        acc += jnp.dot(act, wo_ref[sl, :], preferred_element_type=F32)
    o_ref[...] = x_ref[...] + _gate(mod_ref, 1) * acc


def _ffn(x, mod, norm2, w_in, w_out, rows_per_batch, tm):
    r = x.shape[0]
    tpb = rows_per_batch // tm
    row = lambda i: (i, 0)
    const = lambda i: (0, 0)
    return pl.pallas_call(
        _ffn_kernel, grid=(r // tm,),
        in_specs=[pl.BlockSpec((tm, D_MODEL), row), pl.BlockSpec((1, 1, 6 * D_MODEL), lambda i: (i // tpb, 0, 0)),
                  pl.BlockSpec((1, D_MODEL), const), pl.BlockSpec(w_in.shape, const), pl.BlockSpec(w_out.shape, const)],
        out_specs=pl.BlockSpec((tm, D_MODEL), row),
        out_shape=jax.ShapeDtypeStruct(x.shape, F32),
        compiler_params=_params(("arbitrary",)), name="ffn",
    )(x, mod, norm2, w_in, w_out)


def _rope_tables(n_tokens):
    t = jnp.arange(n_tokens)
    pos = jnp.stack([(t // GRID_W).astype(F32), (t % GRID_W).astype(F32)], axis=1)
    freqs = ROPE_THETA ** (-jnp.arange(ROPE_PAIRS, dtype=F32) / ROPE_PAIRS)
    ang = pos[:, :, None] * freqs
    cos, sin = jnp.cos(ang), jnp.sin(ang)
    zero = jnp.zeros_like(sin)
    lay = lambda first, second: jnp.tile(jnp.stack([first, second], axis=2).reshape(n_tokens, HEAD_DIM), (1, 2))
    return lay(cos, cos), lay(-sin, zero), lay(zero, sin)


def _with_ones(v):
    rows, width = v.shape
    v3 = v.reshape(rows, width // LANES, LANES).astype(BF16)
    return jnp.concatenate([v3, jnp.ones_like(v3)], axis=-1).reshape(rows, 2 * width)


def _trunk(x, mods, li, P, batch, rope_tabs, ctx, tm, prev=None, n_layers=1):
    lj = li // 2
    rows_per_batch = x.shape[0] // batch
    mod_rpb = rows_per_batch if mods.shape[0] > 1 else x.shape[0]
    is_ctx = ctx is None
    new = None
    if li % 2 == 0:
        lam_init = 0.8 - 0.6 * math.exp(-0.3 * li)
        outs = _even_inproj(x, mods, P['norm1'][li], P['w_even_main'][lj], P['b_gate'][lj],
                            P['gq_even'][lj], P['gk_even'][lj], rope_tabs, mod_rpb,
                            (rows_per_batch, lj, n_layers, prev[:2] if prev else None) if is_ctx else None, tm)
        q, k, v, mq, mk, mv, mo, gt = outs[:8]
        ck, cv, init = (None, None, None) if is_ctx else ctx
        if is_ctx:
            da = _da_ctx_attention(q, k, v, P['lam_even'][lj], P['da_norm'][lj], batch, lam_init)
        else:
            da = _pipe_attention("da", q, k, v, ck, cv, batch, ATTN_TQ,
                                 extra=(P['lam_even'][lj], P['da_norm'][lj]), lam_init=lam_init)
        hf, hb, c_n, n_n, m_n = _mlstm(mq, mk, mv, gt, init, batch, lj if is_ctx else 0, n_layers,
                                       prev[2:] if prev else None)
        x = _even_outproj(da, hf, hb, mo, P['ml_norm'][lj], P['w_out_even'][lj], x, mods, mod_rpb, tm)
        if is_ctx:
            new = (outs[8], outs[9], c_n, n_n, m_n)
    else:
        outs = _odd_inproj(x, mods, P['norm1'][li], P['w_in_odd'][lj], P['gq_odd'][lj], P['gk_odd'][lj],
                           rope_tabs, mod_rpb, (rows_per_batch, lj, n_layers, prev) if is_ctx else None, tm)
        q, k, v = outs[:3]
        ck, cv = (None, None) if is_ctx else ctx
        if is_ctx:
            a = _gqa_ctx_attention(q, k, v, batch)
        else:
            a = _pipe_attention("gqa", q, k, v, ck, cv, batch, ATTN_TQ)
        x = _odd_outproj(a, P['w_out_odd'][lj], x, mods, mod_rpb, tm)
        if is_ctx:
            new = (outs[3], outs[4])
    x = _ffn(x, mods, P['norm2'][li], P['w_ffn_in'][li], P['w_ffn_out'][li], mod_rpb, tm)
    return x, new


def kernel(x_prompt, x_sample, c, cache_da_k, cache_da_v, state_mlstm_C, state_mlstm_n, state_mlstm_m, cache_gqa_k, cache_gqa_v, c_ctx, norm1, norm2, w_mod, b_mod, w_in_even, b_gate_even, qk_gain_even, lam_even, da_norm_even, ml_norm_even, w_out_even, w_in_odd, qk_gain_odd, w_out_odd, w_ffn_in, w_ffn_out):
    batch, seq, _ = x_prompt.shape
    dbatch, dseq, _ = x_sample.shape
    depth = norm1.shape[0]
    n_even, n_odd = w_in_even.shape[0], w_in_odd.shape[0]
    past = cache_da_k.shape[2]
    tm = 512

    P = dict(
        norm1=norm1.reshape(depth, 1, D_MODEL), norm2=norm2.reshape(depth, 1, D_MODEL),
        w_even_main=jnp.pad(w_in_even, ((0, 0), (0, 0), (0, LANES - N_GATES))).astype(BF16),
        b_gate=jnp.pad(b_gate_even.reshape(n_even, 1, N_GATES), ((0, 0), (0, 0), (0, LANES - N_GATES))),
        gq_even=qk_gain_even[:, 0].reshape(n_even, 1, LANES), gk_even=qk_gain_even[:, 1].reshape(n_even, 1, LANES),
        lam_even=lam_even, da_norm=da_norm_even.reshape(n_even, 1, LANES),
        ml_norm=ml_norm_even.reshape(n_even, 1, ML_DIM),
        w_out_even=w_out_even.astype(BF16), w_in_odd=w_in_odd.astype(BF16),
        gq_odd=jnp.tile(qk_gain_odd[:, 0], (1, 2)).reshape(n_odd, 1, LANES),
        gk_odd=jnp.tile(qk_gain_odd[:, 1], (1, 2)).reshape(n_odd, 1, LANES),
        w_out_odd=w_out_odd.astype(BF16), w_ffn_in=w_ffn_in.astype(BF16), w_ffn_out=w_ffn_out.astype(BF16))

    cond8 = jnp.concatenate([c_ctx[None, :], c, jnp.zeros((8 - 1 - dbatch, D_MODEL), F32)], axis=0)
    mod = _modulation(cond8, w_mod, b_mod)
    rope_tabs = _rope_tables(dseq)

    x = x_prompt.reshape(batch * seq, D_MODEL)
    new_even = new_odd = None
    for li in range(depth):
        mods = mod[li, 0:1].reshape(1, 1, 6 * D_MODEL)
        if li % 2 == 0:
            x, new_even = _trunk(x, mods, li, P, batch, None, None, tm, new_even, n_even)
        else:
            x, new_odd = _trunk(x, mods, li, P, batch, None, None, tm, new_odd, n_odd)
    y_prompt = x.reshape(batch, seq, D_MODEL)

    x = x_sample.reshape(dbatch * dseq, D_MODEL)
    for li in range(depth):
        j = li // 2
        if li % 2 == 0:
            ctx = (cache_da_k[:, j].reshape(dbatch * past, SEG).astype(BF16),
                   _with_ones(cache_da_v[:, j].reshape(dbatch * past, SEG)),
                   (state_mlstm_C[:, j], state_mlstm_n[:, j][:, :, :, None, :],
                    jnp.broadcast_to(state_mlstm_m[:, j][:, :, :, None, None], (dbatch, 2, ML_HEADS, 1, ML_DIM))))
        else:
            ctx = (cache_gqa_k[:, j].reshape(dbatch * past, GQA_KV_HEADS * HEAD_DIM).astype(BF16),
                   _with_ones(cache_gqa_v[:, j].reshape(dbatch * past, GQA_KV_HEADS * HEAD_DIM)))
        x, _ = _trunk(x, mod[li, 1:1 + dbatch].reshape(dbatch, 1, 6 * D_MODEL), li, P, dbatch, rope_tabs, ctx, tm)
    y_sample = x.reshape(dbatch, dseq, D_MODEL)

    new_da_k = new_even[0].reshape(batch, n_even, seq, DA_HEADS, 2, HEAD_DIM)
    new_da_v = new_even[1].reshape(batch, n_even, seq, DA_HEADS, 2 * HEAD_DIM)
    new_c = new_even[2]
    new_n = new_even[3][:, :, :, :, 0, :]
    new_m = new_even[4][:, :, :, :, 0, 0]
    new_gqa_k = new_odd[0].reshape(batch, n_odd, seq, GQA_KV_HEADS, HEAD_DIM)
    new_gqa_v = new_odd[1].reshape(batch, n_odd, seq, GQA_KV_HEADS, HEAD_DIM)
    return (y_prompt, y_sample, new_da_k, new_da_v, new_c, new_n, new_m, new_gqa_k, new_gqa_v)
```

```python
import functools
import math

import jax
import jax.numpy as jnp
from jax import lax
from jax.experimental import pallas as pl
from jax.experimental.pallas import tpu as pltpu

F32 = jnp.float32
BF16 = jnp.bfloat16
HIGHEST = lax.Precision.HIGHEST

D_MODEL = 1024
GRID_W = 64
HEAD_DIM = 64
ROPE_PAIRS = HEAD_DIM // 4
ROPE_THETA = 10000.0
EPS = 1e-6
DA_HEADS = 4
ML_HEADS = 4
ML_DIM = 128
N_GATES = 2 * 2 * ML_HEADS
GQA_Q_HEADS = 16
GQA_KV_HEADS = 4
SEG = 512
EVEN_MAIN = 7 * SEG
D_FF = -(-8 * D_MODEL // (3 * 256)) * 256

LANES = 128
VMEM_LIMIT = 56 * 2 ** 20

ATTN_TQ = 256
ATTN_TK = 256
ATTN_LAG = 3
Q_SCALE = math.log2(math.e) * HEAD_DIM ** -0.5
ML_CHUNK = 128
FF_CHUNK = 256
ML_PAD = 8


def _params(sem, vmem=VMEM_LIMIT):
    return pltpu.CompilerParams(dimension_semantics=sem, vmem_limit_bytes=vmem)


def _round_robin(generators):
    results = [None] * len(generators)
    live = list(enumerate(generators))
    while live:
        still = []
        for idx, gen in live:
            try:
                next(gen)
                still.append((idx, gen))
            except StopIteration as stop:
                results[idx] = stop.value
        live = still
    return results


def _rms_rows(x, g):
    return x * lax.rsqrt(jnp.mean(x * x, axis=-1, keepdims=True) + EPS) * g


def _lane_lo():
    return lax.broadcasted_iota(jnp.int32, (1, LANES), 1) < HEAD_DIM


def _half_rms(x, gain, lo):
    x2 = x * x
    s_lo = jnp.sum(jnp.where(lo, x2, 0.0), axis=-1, keepdims=True)
    s_hi = jnp.sum(jnp.where(lo, 0.0, x2), axis=-1, keepdims=True)
    r = jnp.where(lo, lax.rsqrt(s_lo * (1.0 / HEAD_DIM) + EPS), lax.rsqrt(s_hi * (1.0 / HEAD_DIM) + EPS))
    return x * r * gain


def _half_indicator():
    r = lax.broadcasted_iota(jnp.int32, (LANES, LANES), 0) // HEAD_DIM
    c = lax.broadcasted_iota(jnp.int32, (LANES, LANES), 1) // HEAD_DIM
    return jnp.where(r == c, 1.0, 0.0).astype(BF16)


def _half_rms_mxu(x, gain, ind):
    ss = jnp.dot((x * x).astype(BF16), ind, preferred_element_type=F32)
    return x * lax.rsqrt(ss * (1.0 / HEAD_DIM) + EPS) * gain


def _rope(y, cos, sin_a, sin_b):
    return y * cos + pltpu.roll(y, LANES - ROPE_PAIRS, 1) * sin_a + pltpu.roll(y, ROPE_PAIRS, 1) * sin_b


def _sigmoid(x):
    return 1.0 / (1.0 + jnp.exp(-x))


def _store_cache(ref, cols, y, seq):
    for b in range(y.shape[0] // seq):
        ref[b, 0, :, cols] = y[b * seq:(b + 1) * seq]


def _cache_outputs(cache, rows, width, tm):
    seq, layer, n_layers, prev = cache
    shape = jax.ShapeDtypeStruct((rows // seq, n_layers, seq, width), F32)
    spec = pl.BlockSpec((tm // seq, 1, seq, width), lambda i: (i, layer, 0, 0))
    return shape, spec


def _store_with_ones(v_o, v):
    ones = jnp.ones((v.shape[0], LANES), BF16)
    for g in range(v.shape[1] // LANES):
        v_o[:, 2 * g * LANES:(2 * g + 1) * LANES] = v[:, g * LANES:(g + 1) * LANES].astype(BF16)
        v_o[:, (2 * g + 1) * LANES:(2 * g + 2) * LANES] = ones


def _mod_kernel(cond_ref, w_ref, b_ref, o_ref):
    c = cond_ref[...]
    a = c * _sigmoid(c)
    o_ref[0] = jnp.dot(a, w_ref[0], preferred_element_type=F32, precision=HIGHEST) + b_ref[0]


def _modulation(cond8, w_mod, b_mod):
    depth, _, n = w_mod.shape
    tn = 1024
    return pl.pallas_call(
        _mod_kernel,
        grid=(depth, n // tn),
        in_specs=[pl.BlockSpec((8, D_MODEL), lambda l, j: (0, 0)),
                  pl.BlockSpec((1, D_MODEL, tn), lambda l, j: (l, 0, j)),
                  pl.BlockSpec((1, 1, tn), lambda l, j: (l, 0, j))],
        out_specs=pl.BlockSpec((1, 8, tn), lambda l, j: (l, 0, j)),
        out_shape=jax.ShapeDtypeStruct((depth, 8, n), F32),
        compiler_params=_params(("arbitrary", "arbitrary")), name="modulation",
    )(cond8, w_mod, b_mod.reshape(depth, 1, n))


def _modulated(x_ref, mod_ref, norm_ref, which):
    sh = mod_ref[0, :, (3 * which) * D_MODEL:(3 * which + 1) * D_MODEL]
    sc = mod_ref[0, :, (3 * which + 1) * D_MODEL:(3 * which + 2) * D_MODEL]
    return _rms_rows(x_ref[...], norm_ref[...]) * (1.0 + sc) + sh


def _even_inproj_kernel(*refs, rope, cache, n_prev):
    it = iter(refs)
    x_ref, mod_ref, n1_ref, w_ref, bg_ref, gq_ref, gk_ref = [next(it) for _ in range(7)]
    if rope:
        cos_ref, sa_ref, sb_ref = [next(it) for _ in range(3)]
    for _ in range(n_prev):
        next(it)
    q_o, k_o, v_o, mq_o, mk_o, mv_o, mo_o, gt_o = [next(it) for _ in range(8)]
    if cache:
        kf_o, vf_o = [next(it) for _ in range(2)]

    hb = _modulated(x_ref, mod_ref, n1_ref, 0).astype(BF16)
    lo = _lane_lo()

    def seg(i):
        return jnp.dot(hb, w_ref[:, i * SEG:(i + 1) * SEG], preferred_element_type=F32)

    def qk_norm(p, gain_ref, hh):
        y = _half_rms(p[:, hh * LANES:(hh + 1) * LANES], gain_ref[...], lo)
        if rope:
            y = _rope(y, cos_ref[...], sa_ref[...], sb_ref[...])
        return y

    p = seg(0)
    for hh in range(DA_HEADS):
        q_o[:, hh * LANES:(hh + 1) * LANES] = (qk_norm(p, gq_ref, hh) * Q_SCALE).astype(BF16)
    p = seg(1)
    for hh in range(DA_HEADS):
        sl = slice(hh * LANES, (hh + 1) * LANES)
        y = qk_norm(p, gk_ref, hh)
        if cache:
            _store_cache(kf_o, sl, y, cache)
        k_o[:, sl] = y.astype(BF16)
    p = seg(2)
    if cache:
        v_o[...] = p.astype(BF16)
        _store_cache(vf_o, slice(None), p, cache)
    else:
        _store_with_ones(v_o, p)
    mq_o[...] = seg(3).astype(BF16)
    mk_o[...] = (seg(4) * (ML_DIM ** -0.5)).astype(BF16)
    mv_o[...] = seg(5).astype(BF16)
    mo_o[...] = _sigmoid(seg(6)).astype(BF16)
    pg = jnp.dot(hb, w_ref[:, EVEN_MAIN:EVEN_MAIN + LANES], preferred_element_type=F32) + bg_ref[...]
    gt_o[...] = pg.T[:N_GATES, :]


def _even_inproj(x, mod, norm1, w_main, b_gate, gain_q, gain_k, rope_tabs, rows_per_batch, cache, tm):
    r = x.shape[0]
    nt = r // tm
    tpb = rows_per_batch // tm
    row = lambda i: (i, 0)
    const = lambda i: (0, 0)
    in_specs = [pl.BlockSpec((tm, D_MODEL), row),
                pl.BlockSpec((1, 1, 6 * D_MODEL), lambda i: (i // tpb, 0, 0)),
                pl.BlockSpec((1, D_MODEL), const),
                pl.BlockSpec((D_MODEL, EVEN_MAIN + LANES), const),
                pl.BlockSpec((1, LANES), const),
                pl.BlockSpec((1, LANES), const),
                pl.BlockSpec((1, LANES), const)]
    args = [x, mod, norm1, w_main, b_gate, gain_q, gain_k]
    if rope_tabs is not None:
        in_specs += [pl.BlockSpec((tm, LANES), lambda i: (i % tpb, 0))] * 3
        args += list(rope_tabs)
    out_shape = [jax.ShapeDtypeStruct((r, SEG), BF16)] * 7
    out_specs = [pl.BlockSpec((tm, SEG), row)] * 7
    if not cache:
        out_shape[2] = jax.ShapeDtypeStruct((r, 2 * SEG), BF16)
        out_specs[2] = pl.BlockSpec((tm, 2 * SEG), row)
    out_shape += [jax.ShapeDtypeStruct((N_GATES, r), F32)]
    out_specs += [pl.BlockSpec((N_GATES, tm), lambda i: (0, i))]
    aliases = {}
    if cache:
        shape, spec = _cache_outputs(cache, r, SEG, tm)
        for prev in cache[3] or ():
            aliases[len(args)] = len(out_shape)
            in_specs.append(pl.BlockSpec(memory_space=pl.ANY))
            args.append(prev)
            out_shape.append(shape)
            out_specs.append(spec)
        if not cache[3]:
            out_shape += [shape] * 2
            out_specs += [spec] * 2
    return pl.pallas_call(
        functools.partial(_even_inproj_kernel, rope=rope_tabs is not None, cache=cache[0] if cache else None,
                          n_prev=len(aliases)),
        grid=(nt,), in_specs=in_specs, out_specs=out_specs, out_shape=out_shape, input_output_aliases=aliases,
        compiler_params=_params(("arbitrary",)), name="even_inproj_ctx" if cache else "even_inproj_dec",
    )(*args)


def _odd_inproj_kernel(*refs, rope, cache, n_prev):
    it = iter(refs)
    x_ref, mod_ref, n1_ref, w_ref, gq_ref, gk_ref = [next(it) for _ in range(6)]
    if rope:
        cos_ref, sa_ref, sb_ref = [next(it) for _ in range(3)]
    for _ in range(n_prev):
        next(it)
    q_o, k_o, v_o = [next(it) for _ in range(3)]
    if cache:
        kf_o, vf_o = [next(it) for _ in range(2)]

    hb = _modulated(x_ref, mod_ref, n1_ref, 0).astype(BF16)
    ind = _half_indicator() if rope else None
    lo = _lane_lo()
    nq = GQA_Q_HEADS * HEAD_DIM
    nkv = GQA_KV_HEADS * HEAD_DIM

    def qk_norm(t, gain_ref):
        y = _half_rms_mxu(t, gain_ref[...], ind) if rope else _half_rms(t, gain_ref[...], lo)
        if rope:
            y = _rope(y, cos_ref[...], sa_ref[...], sb_ref[...])
        return y

    for c in range(nq // SEG):
        p = jnp.dot(hb, w_ref[:, c * SEG:(c + 1) * SEG], preferred_element_type=F32)
        for hh in range(SEG // LANES):
            sl = slice(c * SEG + hh * LANES, c * SEG + (hh + 1) * LANES)
            q_o[:, sl] = (qk_norm(p[:, hh * LANES:(hh + 1) * LANES], gq_ref) * Q_SCALE).astype(BF16)
    p = jnp.dot(hb, w_ref[:, nq:nq + 2 * nkv], preferred_element_type=F32)
    for hh in range(nkv // LANES):
        sl = slice(hh * LANES, (hh + 1) * LANES)
        y = qk_norm(p[:, sl], gk_ref)
        if cache:
            _store_cache(kf_o, sl, y, cache)
        k_o[:, sl] = y.astype(BF16)
    v = p[:, nkv:2 * nkv]
    if cache:
        v_o[...] = v.astype(BF16)
        _store_cache(vf_o, slice(None), v, cache)
    else:
        _store_with_ones(v_o, v)


def _odd_inproj(x, mod, norm1, w, gain_q, gain_k, rope_tabs, rows_per_batch, cache, tm):
    r = x.shape[0]
    nt = r // tm
    tpb = rows_per_batch // tm
    nq = GQA_Q_HEADS * HEAD_DIM
    nkv = GQA_KV_HEADS * HEAD_DIM
    row = lambda i: (i, 0)
    const = lambda i: (0, 0)
    in_specs = [pl.BlockSpec((tm, D_MODEL), row),
                pl.BlockSpec((1, 1, 6 * D_MODEL), lambda i: (i // tpb, 0, 0)),
                pl.BlockSpec((1, D_MODEL), const),
                pl.BlockSpec((D_MODEL, nq + 2 * nkv), const),
                pl.BlockSpec((1, LANES), const),
                pl.BlockSpec((1, LANES), const)]
    args = [x, mod, norm1, w, gain_q, gain_k]
    if rope_tabs is not None:
        in_specs += [pl.BlockSpec((tm, LANES), lambda i: (i % tpb, 0))] * 3
        args += list(rope_tabs)
    out_shape = [jax.ShapeDtypeStruct((r, nq), BF16), jax.ShapeDtypeStruct((r, nkv), BF16),
                 jax.ShapeDtypeStruct((r, nkv), BF16)]
    out_specs = [pl.BlockSpec((tm, nq), row), pl.BlockSpec((tm, nkv), row), pl.BlockSpec((tm, nkv), row)]
    if not cache:
        out_shape[2] = jax.ShapeDtypeStruct((r, 2 * nkv), BF16)
        out_specs[2] = pl.BlockSpec((tm, 2 * nkv), row)
    aliases = {}
    if cache:
        shape, spec = _cache_outputs(cache, r, nkv, tm)
        for prev in cache[3] or ():
            aliases[len(args)] = len(out_shape)
            in_specs.append(pl.BlockSpec(memory_space=pl.ANY))
            args.append(prev)
            out_shape.append(shape)
            out_specs.append(spec)
        if not cache[3]:
            out_shape += [shape] * 2
            out_specs += [spec] * 2
    return pl.pallas_call(
        functools.partial(_odd_inproj_kernel, rope=rope_tabs is not None, cache=cache[0] if cache else None,
                          n_prev=len(aliases)),
        grid=(nt,), in_specs=in_specs, out_specs=out_specs, out_shape=out_shape, input_output_aliases=aliases,
        compiler_params=_params(("arbitrary",)), name="odd_inproj_ctx" if cache else "odd_inproj_dec",
    )(*args)


_NT = (((1,), (1,)), ((), ()))


def _attend(q, k, v):
    s = lax.dot_general(q, k, _NT, preferred_element_type=F32)
    yield
    p = jnp.exp2(s - jnp.max(s, axis=-1, keepdims=True))
    l = jnp.sum(p, axis=-1, keepdims=True)
    yield
    acc = jnp.dot(p.astype(BF16), v, preferred_element_type=F32)
    yield
    return acc / l


def _lam_full(lam_ref, lam_init):
    lam = lam_ref[...]
    return (jnp.exp(jnp.sum(lam[0:1] * lam[1:2], axis=-1, keepdims=True))
            - jnp.exp(jnp.sum(lam[2:3] * lam[3:4], axis=-1, keepdims=True)) + lam_init)


def _da_ctx_kernel(q_ref, k_ref, v_ref, lam_ref, gn_ref, o_ref, *, lam_init):
    lo = _lane_lo()
    lam_full = _lam_full(lam_ref, lam_init)
    chains = []
    for h in range(DA_HEADS):
        sl = slice(h * LANES, (h + 1) * LANES)
        q, k, v = q_ref[:, sl], k_ref[:, sl], v_ref[:, sl]
        zero = jnp.zeros_like(q)
        chains += [_attend(jnp.where(lo, q, zero), k, v), _attend(jnp.where(lo, zero, q), k, v)]
    maps = _round_robin(chains)
    for h in range(DA_HEADS):
        o = maps[2 * h] - lam_full * maps[2 * h + 1]
        o_ref[:, h * LANES:(h + 1) * LANES] = (_rms_rows(o, gn_ref[...]) * (1.0 - lam_init)).astype(BF16)


def _da_ctx_attention(q, k, v, lam, gain, batch, lam_init):
    r = q.shape[0]
    s = r // batch
    blk = pl.BlockSpec((s, SEG), lambda b: (b, 0))
    const = lambda b: (0, 0)
    return pl.pallas_call(
        functools.partial(_da_ctx_kernel, lam_init=lam_init), grid=(batch,),
        in_specs=[blk, blk, blk, pl.BlockSpec(lam.shape, const), pl.BlockSpec((1, LANES), const)],
        out_specs=blk, out_shape=jax.ShapeDtypeStruct((r, SEG), BF16),
        compiler_params=_params(("arbitrary",)), name="da_attn_ctx",
    )(q, k, v, lam, gain)


def _gqa_ctx_kernel(q_ref, k_ref, v_ref, o_ref):
    lo = _lane_lo()
    group_pairs = GQA_Q_HEADS // GQA_KV_HEADS // 2
    chains, keeps = [], []
    for j in range(GQA_KV_HEADS // 2):
        k, v = k_ref[:, j * LANES:(j + 1) * LANES], v_ref[:, j * LANES:(j + 1) * LANES]
        for half in range(2):
            keep = lo if half == 0 else jnp.logical_not(lo)
            for pr in range(group_pairs):
                col = ((j * 2 + half) * group_pairs + pr) * LANES
                qp = q_ref[:, col:col + LANES].astype(F32)
                qr = pltpu.roll(qp, HEAD_DIM, 1)
                chains.append(_attend(jnp.where(keep, jnp.where(lo, qp, qr), 0.0).astype(BF16), k, v))
                chains.append(_attend(jnp.where(keep, jnp.where(lo, qr, qp), 0.0).astype(BF16), k, v))
                keeps.append(keep)
    heads = _round_robin(chains)
    for pair, keep in enumerate(keeps):
        oa, ob = heads[2 * pair], heads[2 * pair + 1]
        first = jnp.where(keep, oa, pltpu.roll(oa, HEAD_DIM, 1))
        second = jnp.where(keep, ob, pltpu.roll(ob, HEAD_DIM, 1))
        o_ref[:, pair * LANES:(pair + 1) * LANES] = jnp.where(lo, first, second).astype(BF16)


def _gqa_ctx_attention(q, k, v, batch):
    r = q.shape[0]
    s = r // batch
    row = lambda b: (b, 0)
    return pl.pallas_call(
        _gqa_ctx_kernel, grid=(batch,),
        in_specs=[pl.BlockSpec((s, q.shape[1]), row), pl.BlockSpec((s, k.shape[1]), row),
                  pl.BlockSpec((s, v.shape[1]), row)],
        out_specs=pl.BlockSpec((s, q.shape[1]), row), out_shape=jax.ShapeDtypeStruct(q.shape, BF16),
        compiler_params=_params(("arbitrary",)), name="gqa_attn_ctx",
    )(q, k, v)


def _key_chunks(ref, cache_ref):
    chunks, col = [], 0
    for r in (ref, cache_ref):
        for c in range(r.shape[0] // ATTN_TK):
            chunks.append((r, c * ATTN_TK, col))
            col += ATTN_TK
    return chunks


def _score_head(q, k_ref, ck_ref, s_w, m_w, a):
    mx = None
    for ref, off, col in _key_chunks(k_ref, ck_ref):
        s = lax.dot_general(q, ref[off:off + ATTN_TK, :], _NT, preferred_element_type=F32)
        s_w[a, :, col:col + ATTN_TK] = s
        tile_max = functools.reduce(jnp.maximum, [s[:, i * LANES:(i + 1) * LANES] for i in range(ATTN_TK // LANES)])
        mx = tile_max if mx is None else jnp.maximum(mx, tile_max)
        yield
    m_w[a] = jnp.max(mx, axis=-1, keepdims=True)


def _finish_head(v_ref, cv_ref, s_r, m_r, a):
    m = m_r[a]
    acc = None
    for ref, off, col in _key_chunks(v_ref, cv_ref):
        p = jnp.exp2(s_r[a, :, col:col + ATTN_TK] - m).astype(BF16)
        part = jnp.dot(p, ref[off:off + ATTN_TK, :], preferred_element_type=F32)
        acc = part if acc is None else acc + part
        yield
    return acc[:, 0:LANES] / acc[:, LANES:LANES + 1]


def _score_and_finish(qs, k_ref, ck_ref, s_w, m_w, v_ref, cv_ref, s_r, m_r):
    def lagged(gen):
        for _ in range(ATTN_LAG):
            yield
        return (yield from gen)

    stages = [_finish_head(v_ref, cv_ref, s_r, m_r, a) for a in range(2)]
    stages += [lagged(_score_head(qs[a], k_ref, ck_ref, s_w, m_w, a)) for a in range(2)]
    results = _round_robin(stages)
    return results[0], results[1]


def _by_parity(t, body, sa, ma, sb, mb):
    @pl.when(t == 0)
    def _():
        sb[...] = jnp.zeros(sb.shape, F32)
        mb[...] = jnp.zeros(mb.shape, F32)

    @pl.when(t % 2 == 0)
    def _():
        body(sa, ma, sb, mb)

    @pl.when(t % 2 == 1)
    def _():
        body(sb, mb, sa, ma)


def _da_pipe_kernel(q_ref, k_ref, ck_ref, v_ref, cv_ref, lam_ref, gn_ref, o_ref, sa, ma, sb, mb, *, lam_init):
    lo = _lane_lo()

    def body(s_w, m_w, s_r, m_r):
        q = q_ref[...]
        zero = jnp.zeros_like(q)
        a1, a2 = _score_and_finish([jnp.where(lo, q, zero), jnp.where(lo, zero, q)], k_ref, ck_ref, s_w, m_w,
                                   v_ref, cv_ref, s_r, m_r)
        o = a1 - _lam_full(lam_ref, lam_init) * a2
        o_ref[...] = (_rms_rows(o, gn_ref[...]) * (1.0 - lam_init)).astype(BF16)

    _by_parity(pl.program_id(0), body, sa, ma, sb, mb)


def _gqa_pipe_kernel(q_ref, k_ref, ck_ref, v_ref, cv_ref, o_ref, sa, ma, sb, mb, *, n_items):
    t = pl.program_id(0)
    lo = _lane_lo()
    q_pairs = GQA_Q_HEADS // GQA_KV_HEADS

    def keep_of(item):
        half = (item % q_pairs) // (q_pairs // 2)
        return jnp.where(lo, 0, 1) == half

    def body(s_w, m_w, s_r, m_r):
        keep = keep_of(jnp.minimum(t, n_items - 1))
        qp = q_ref[...].astype(F32)
        qr = pltpu.roll(qp, HEAD_DIM, 1)
        qa = jnp.where(keep, jnp.where(lo, qp, qr), 0.0).astype(BF16)
        qb = jnp.where(keep, jnp.where(lo, qr, qp), 0.0).astype(BF16)
        oa, ob = _score_and_finish([qa, qb], k_ref, ck_ref, s_w, m_w, v_ref, cv_ref, s_r, m_r)
        keep = keep_of(jnp.maximum(t - 1, 0))
        first = jnp.where(keep, oa, pltpu.roll(oa, HEAD_DIM, 1))
        second = jnp.where(keep, ob, pltpu.roll(ob, HEAD_DIM, 1))
        o_ref[...] = jnp.where(lo, first, second).astype(BF16)

    _by_parity(t, body, sa, ma, sb, mb)


def _pipe_attention(kind, q, k, v1, ck, cv1, batch, tq, extra=(), lam_init=None):
    r = q.shape[0]
    s = r // batch
    past = ck.shape[0] // batch
    nq = s // tq
    groups = k.shape[1] // LANES
    per_q = q.shape[1] // LANES // groups
    per_g = nq * per_q
    n_items = batch * groups * per_g

    def split(item):
        b, g, rest = item // (groups * per_g), (item // per_g) % groups, item % per_g
        return b, g, rest // per_q, rest % per_q

    def q_index(item):
        b, g, i, p = split(item)
        return (b * nq + i, g * per_q + p)

    def kv_index(item):
        b, g, _, _ = split(item)
        return (b, g)

    cur = lambda t: jnp.minimum(t, n_items - 1)
    prev = lambda t: jnp.maximum(t - 1, 0)
    in_specs = [pl.BlockSpec((tq, LANES), lambda t: q_index(cur(t))),
                pl.BlockSpec((s, LANES), lambda t: kv_index(cur(t))),
                pl.BlockSpec((past, LANES), lambda t: kv_index(cur(t))),
                pl.BlockSpec((s, 2 * LANES), lambda t: kv_index(prev(t))),
                pl.BlockSpec((past, 2 * LANES), lambda t: kv_index(prev(t)))]
    in_specs += [pl.BlockSpec(e.shape, lambda t: (0, 0)) for e in extra]
    if kind == "da":
        body = functools.partial(_da_pipe_kernel, lam_init=lam_init)
    else:
        body = functools.partial(_gqa_pipe_kernel, n_items=n_items)
    scratch = [pltpu.VMEM((2, tq, s + past), F32), pltpu.VMEM((2, tq, 1), F32)] * 2
    return pl.pallas_call(
        body, grid=(n_items + 1,), in_specs=in_specs,
        out_specs=pl.BlockSpec((tq, LANES), lambda t: q_index(prev(t))),
        out_shape=jax.ShapeDtypeStruct(q.shape, BF16),
        scratch_shapes=scratch,
        compiler_params=_params(("arbitrary",)),
        name=kind + "_attn_dec",
    )(q, k, ck, v1, cv1, *extra)


def _log_sigmoid(x):
    return jnp.minimum(x, 0.0) - jnp.log1p(jnp.exp(-jnp.abs(x)))


def _mlstm_chunk(q, k, v, gt, c_s, n_s, m_s, head, direction):
    L, D = ML_CHUNK, ML_DIM
    i_idx = direction * 2 * ML_HEADS + head
    f_idx = i_idx + ML_HEADS
    i_row, f_row = gt[i_idx:i_idx + 1, :], gt[f_idx:f_idx + 1, :]
    lf_row = _log_sigmoid(f_row)
    r_i = lax.broadcasted_iota(jnp.int32, (L, L), 0)
    c_i = lax.broadcasted_iota(jnp.int32, (L, L), 1)
    vis = (r_i <= c_i) if direction == 0 else (r_i >= c_i)
    between = (c_i > r_i) if direction == 0 else (c_i < r_i)
    lhs_terms, lf_rem, i_rem = [], lf_row, i_row
    for _ in range(3):
        lf_t, i_t = lf_rem.astype(BF16), i_rem.astype(BF16)
        lf_rem, i_rem = lf_rem - lf_t.astype(F32), i_rem - i_t.astype(F32)
        lf_f, i_f = lf_t.astype(F32), i_t.astype(F32)
        lhs_terms.append(jnp.concatenate([
            jnp.concatenate([jnp.where(between, lf_f, 0.0), jnp.where(r_i == c_i, i_f, 0.0)], axis=1),
            jnp.concatenate([jnp.broadcast_to(lf_f, (ML_PAD, L)), jnp.zeros((ML_PAD, L), F32)], axis=1)],
            axis=0).astype(BF16))
    rhs = jnp.concatenate([jnp.where(vis, 1.0, 0.0), jnp.ones((L, L), F32)], axis=0).astype(BF16)
    lhs, rhs = jnp.concatenate(lhs_terms, axis=1), jnp.concatenate([rhs] * 3, axis=0)
    yield
    d_all = jnp.dot(lhs, rhs, preferred_element_type=F32)
    yield
    b_row = d_all[L:L + 1, :]
    last = slice(L - 1, L) if direction == 0 else slice(0, 1)
    b_last = b_row[:, last]
    m_prev = m_s[:, 0:1]
    dmat = jnp.where(vis, d_all[0:L, :], -jnp.inf)
    inter = b_row + m_prev
    m_t = jnp.maximum(inter, jnp.max(dmat, axis=0, keepdims=True))
    w_intra = jnp.exp(dmat - m_t)
    w_inter = jnp.exp(inter - m_t)
    yield
    a_t = (lax.dot_general(k, q, _NT, preferred_element_type=F32) * w_intra).astype(BF16)
    v_t = v.astype(F32).T
    v1 = jnp.concatenate([v_t, jnp.ones((ML_PAD, L), F32)], axis=0).astype(BF16)
    state = jnp.concatenate([c_s[...], jnp.broadcast_to(n_s[...], (ML_PAD, D))], axis=0)
    yield
    tot = (jnp.dot(v1, a_t, preferred_element_type=F32)
           + w_inter * lax.dot_general(state.astype(BF16), q, _NT, preferred_element_type=F32))
    yield
    h_t = tot[0:D, :] / jnp.maximum(jnp.abs(tot[D:D + 1, :]), jnp.exp(-m_t))
    m_new = m_t[:, last]
    w_s = jnp.exp(b_last - b_row + i_row - m_new)
    decay = jnp.exp(b_last + m_prev - m_new)
    vw = jnp.concatenate([v_t * w_s, jnp.broadcast_to(w_s, (ML_PAD, L))], axis=0).astype(BF16)
    yield
    state = decay * state + jnp.dot(vw, k, preferred_element_type=F32)
    return h_t.T, state[0:D, :], state[D:D + 1, :], jnp.broadcast_to(m_new, (1, D))


def _mlstm_kernel(*refs, has_init, n_prev):
    it = iter(refs)
    sides = [[next(it) for _ in range(4)] for _ in range(2)]
    if has_init:
        c0, n0, m0 = [next(it) for _ in range(3)]
    for _ in range(n_prev):
        next(it)
    h_os = [next(it) for _ in range(2)]
    c_o, n_o, m_o = [next(it) for _ in range(3)]
    c_s, n_s, m_s = [next(it) for _ in range(3)]
    chunk = pl.program_id(1)

    @pl.when(chunk == 0)
    def _():
        if has_init:
            c_s[...] = c0[0]
            n_s[...] = n0[0]
            m_s[...] = m0[0]
        else:
            for ref in (c_s, n_s, m_s):
                ref[...] = jnp.zeros(ref.shape, F32)

    chains = []
    for d, (q_ref, k_ref, v_ref, gt_ref) in enumerate(sides):
        gt = gt_ref[...]
        for h in range(ML_HEADS):
            cols = slice(h * ML_DIM, (h + 1) * ML_DIM)
            chains.append(_mlstm_chunk(q_ref[:, cols], k_ref[:, cols], v_ref[:, cols], gt,
                                       c_s.at[d, h], n_s.at[d, h], m_s.at[d, h], h, d))
    results = _round_robin(chains)
    for d, h_o in enumerate(h_os):
        h_o[...] = jnp.concatenate([r[0] for r in results[d * ML_HEADS:(d + 1) * ML_HEADS]], axis=1)
    c_s[...] = jnp.stack([r[1] for r in results]).reshape(c_s.shape)
    n_s[...] = jnp.stack([r[2] for r in results]).reshape(n_s.shape)
    m_s[...] = jnp.stack([r[3] for r in results]).reshape(m_s.shape)

    @pl.when(chunk == pl.num_programs(1) - 1)
    def _():
        c_o[0, 0] = c_s[...]
        n_o[0, 0] = n_s[...]
        m_o[0, 0] = m_s[...]


def _mlstm(q, k, v, gt, init, batch, layer=0, n_layers=1, prev=None):
    r = q.shape[0]
    L = ML_CHUNK
    nc = r // batch // L
    width = ML_HEADS * ML_DIM
    fwd = lambda b, c: b * nc + c
    bwd = lambda b, c: b * nc + nc - 1 - c

    def side(pos):
        return ([pl.BlockSpec((L, width), lambda b, c: (pos(b, c), 0))] * 3
                + [pl.BlockSpec((N_GATES, L), lambda b, c: (0, pos(b, c)))])

    in_specs = side(fwd) + side(bwd)
    args = [q, k, v, gt] * 2
    state_shapes = [(2, ML_HEADS, ML_DIM, ML_DIM), (2, ML_HEADS, 1, ML_DIM), (2, ML_HEADS, 1, ML_DIM)]
    state_specs = [pl.BlockSpec((1,) + shp, lambda b, c: (b, 0, 0, 0, 0)) for shp in state_shapes]
    if init is not None:
        in_specs += state_specs
        args += list(init)
    aliases = {}
    for j, arr in enumerate(prev or ()):
        aliases[len(args)] = 2 + j
        in_specs.append(pl.BlockSpec(memory_space=pl.ANY))
        args.append(arr)
    out_shape = ([jax.ShapeDtypeStruct((r, width), F32)] * 2
                 + [jax.ShapeDtypeStruct((batch, n_layers) + shp, F32) for shp in state_shapes])
    out_specs = [pl.BlockSpec((L, width), lambda b, c: (fwd(b, c), 0)),
                 pl.BlockSpec((L, width), lambda b, c: (bwd(b, c), 0))]
    out_specs += [pl.BlockSpec((1, 1) + shp, lambda b, c: (b, layer, 0, 0, 0, 0)) for shp in state_shapes]
    return pl.pallas_call(
        functools.partial(_mlstm_kernel, has_init=init is not None, n_prev=len(aliases)),
        grid=(batch, nc), in_specs=in_specs, out_specs=out_specs, out_shape=out_shape, input_output_aliases=aliases,
        scratch_shapes=[pltpu.VMEM(shp, F32) for shp in state_shapes],
        compiler_params=_params(("arbitrary", "arbitrary")),
        name="mlstm_dec" if init is not None else "mlstm_ctx",
    )(*args)


def _gate(mod_ref, which):
    return mod_ref[0, :, (3 * which + 2) * D_MODEL:(3 * which + 3) * D_MODEL]


def _even_outproj_kernel(da_ref, hf_ref, hb_ref, mo_ref, gn_ref, w_ref, x_ref, mod_ref, o_ref):
    ys = []
    for hh in range(ML_HEADS):
        sl = slice(hh * ML_DIM, (hh + 1) * ML_DIM)
        y = _rms_rows(hf_ref[:, sl] + hb_ref[:, sl], gn_ref[...]) * mo_ref[:, sl].astype(F32)
        ys.append(y.astype(BF16))
    mix = jnp.concatenate([da_ref[...]] + ys, axis=1)
    acc = jnp.dot(mix, w_ref[...], preferred_element_type=F32)
    o_ref[...] = x_ref[...] + _gate(mod_ref, 0) * acc


def _even_outproj(da, hf, hb, mo, gain, w, x, mod, rows_per_batch, tm):
    r = x.shape[0]
    tpb = rows_per_batch // tm
    row = lambda i: (i, 0)
    const = lambda i: (0, 0)
    return pl.pallas_call(
        _even_outproj_kernel, grid=(r // tm,),
        in_specs=[pl.BlockSpec((tm, SEG), row)] * 4 + [
            pl.BlockSpec((1, ML_DIM), const), pl.BlockSpec(w.shape, const),
            pl.BlockSpec((tm, D_MODEL), row), pl.BlockSpec((1, 1, 6 * D_MODEL), lambda i: (i // tpb, 0, 0))],
        out_specs=pl.BlockSpec((tm, D_MODEL), row),
        out_shape=jax.ShapeDtypeStruct(x.shape, F32),
        compiler_params=_params(("arbitrary",)), name="even_outproj",
    )(da, hf, hb, mo, gain, w, x, mod)


def _odd_outproj_kernel(a_ref, w_ref, x_ref, mod_ref, o_ref):
    acc = jnp.dot(a_ref[...], w_ref[...], preferred_element_type=F32)
    o_ref[...] = x_ref[...] + _gate(mod_ref, 0) * acc


def _odd_outproj(a, w, x, mod, rows_per_batch, tm):
    r = x.shape[0]
    tpb = rows_per_batch // tm
    row = lambda i: (i, 0)
    return pl.pallas_call(
        _odd_outproj_kernel, grid=(r // tm,),
        in_specs=[pl.BlockSpec((tm, a.shape[1]), row), pl.BlockSpec(w.shape, lambda i: (0, 0)),
                  pl.BlockSpec((tm, D_MODEL), row), pl.BlockSpec((1, 1, 6 * D_MODEL), lambda i: (i // tpb, 0, 0))],
        out_specs=pl.BlockSpec((tm, D_MODEL), row),
        out_shape=jax.ShapeDtypeStruct(x.shape, F32),
        compiler_params=_params(("arbitrary",)), name="odd_outproj",
    )(a, w, x, mod)


def _ffn_kernel(x_ref, mod_ref, n2_ref, wi_ref, wo_ref, o_ref):
    hb = _modulated(x_ref, mod_ref, n2_ref, 1).astype(BF16)
    n_chunks = D_FF // FF_CHUNK

    def gate_up(c):
        return (jnp.dot(hb, wi_ref[:, c * FF_CHUNK:(c + 1) * FF_CHUNK], preferred_element_type=F32),
                jnp.dot(hb, wi_ref[:, D_FF + c * FF_CHUNK:D_FF + (c + 1) * FF_CHUNK], preferred_element_type=F32))

    acc = None
    ahead = gate_up(0)
    for c in range(n_chunks):
        gate, up = ahead
        if c + 1 < n_chunks:
            ahead = gate_up(c + 1)
        act = (gate * _sigmoid(gate) * up).astype(BF16)
        part = jnp.dot(act, wo_ref[c * FF_CHUNK:(c + 1) * FF_CHUNK, :], preferred_element_type=F32)
        acc = part if acc is None else acc + part
    o_ref[...] = x_ref[...] + _gate(mod_ref, 1) * acc


def _ffn(x, mod, norm2, w_in, w_out, rows_per_batch, tm):
    r = x.shape[0]
    tpb = rows_per_batch // tm
    row = lambda i: (i, 0)
    const = lambda i: (0, 0)
    return pl.pallas_call(
        _ffn_kernel, grid=(r // tm,),
        in_specs=[pl.BlockSpec((tm, D_MODEL), row), pl.BlockSpec((1, 1, 6 * D_MODEL), lambda i: (i // tpb, 0, 0)),
                  pl.BlockSpec((1, D_MODEL), const), pl.BlockSpec(w_in.shape, const), pl.BlockSpec(w_out.shape, const)],
        out_specs=pl.BlockSpec((tm, D_MODEL), row),
        out_shape=jax.ShapeDtypeStruct(x.shape, F32),
        compiler_params=_params(("arbitrary",)), name="ffn",
    )(x, mod, norm2, w_in, w_out)


def _rope_tables(n_tokens):
    t = jnp.arange(n_tokens)
    pos = jnp.stack([(t // GRID_W).astype(F32), (t % GRID_W).astype(F32)], axis=1)
    freqs = ROPE_THETA ** (-jnp.arange(ROPE_PAIRS, dtype=F32) / ROPE_PAIRS)
    ang = pos[:, :, None] * freqs
    cos, sin = jnp.cos(ang), jnp.sin(ang)
    zero = jnp.zeros_like(sin)
    lay = lambda first, second: jnp.tile(jnp.stack([first, second], axis=2).reshape(n_tokens, HEAD_DIM), (1, 2))
    return lay(cos, cos), lay(-sin, zero), lay(zero, sin)


def _with_ones(v):
    rows, width = v.shape
    v3 = v.reshape(rows, width // LANES, LANES).astype(BF16)
    return jnp.concatenate([v3, jnp.ones_like(v3)], axis=-1).reshape(rows, 2 * width)


def _trunk(x, mods, li, P, batch, rope_tabs, ctx, tm, prev=None, n_layers=1):
    lj = li // 2
    rows_per_batch = x.shape[0] // batch
    mod_rpb = rows_per_batch if mods.shape[0] > 1 else x.shape[0]
    is_ctx = ctx is None
    tm_wide = min(2 * tm, x.shape[0])
    new = None
    if li % 2 == 0:
        lam_init = 0.8 - 0.6 * math.exp(-0.3 * li)
        outs = _even_inproj(x, mods, P['norm1'][li], P['w_even_main'][lj], P['b_gate'][lj],
                            P['gq_even'][lj], P['gk_even'][lj], rope_tabs, mod_rpb,
                            (rows_per_batch, lj, n_layers, prev[:2] if prev else None) if is_ctx else None, tm)
        q, k, v, mq, mk, mv, mo, gt = outs[:8]
        ck, cv, init = (None, None, None) if is_ctx else ctx
        if is_ctx:
            da = _da_ctx_attention(q, k, v, P['lam_even'][lj], P['da_norm'][lj], batch, lam_init)
        else:
            da = _pipe_attention("da", q, k, v, ck, cv, batch, ATTN_TQ,
                                 extra=(P['lam_even'][lj], P['da_norm'][lj]), lam_init=lam_init)
        hf, hb, c_n, n_n, m_n = _mlstm(mq, mk, mv, gt, init, batch, lj if is_ctx else 0, n_layers,
                                       prev[2:] if prev else None)
        x = _even_outproj(da, hf, hb, mo, P['ml_norm'][lj], P['w_out_even'][lj], x, mods, mod_rpb, tm_wide)
        if is_ctx:
            new = (outs[8], outs[9], c_n, n_n, m_n)
    else:
        outs = _odd_inproj(x, mods, P['norm1'][li], P['w_in_odd'][lj], P['gq_odd'][lj], P['gk_odd'][lj],
                           rope_tabs, mod_rpb, (rows_per_batch, lj, n_layers, prev) if is_ctx else None, tm)
        q, k, v = outs[:3]
        ck, cv = (None, None) if is_ctx else ctx
        if is_ctx:
            a = _gqa_ctx_attention(q, k, v, batch)
        else:
            a = _pipe_attention("gqa", q, k, v, ck, cv, batch, ATTN_TQ)
        x = _odd_outproj(a, P['w_out_odd'][lj], x, mods, mod_rpb, tm_wide)
        if is_ctx:
            new = (outs[3], outs[4])
    x = _ffn(x, mods, P['norm2'][li], P['w_ffn_in'][li], P['w_ffn_out'][li], mod_rpb, tm_wide)
    return x, new


def kernel(x_prompt, x_sample, c, cache_da_k, cache_da_v, state_mlstm_C, state_mlstm_n, state_mlstm_m, cache_gqa_k, cache_gqa_v, c_ctx, norm1, norm2, w_mod, b_mod, w_in_even, b_gate_even, qk_gain_even, lam_even, da_norm_even, ml_norm_even, w_out_even, w_in_odd, qk_gain_odd, w_out_odd, w_ffn_in, w_ffn_out):
    batch, seq, _ = x_prompt.shape
    dbatch, dseq, _ = x_sample.shape
    depth = norm1.shape[0]
    n_even, n_odd = w_in_even.shape[0], w_in_odd.shape[0]
    past = cache_da_k.shape[2]
    tm = 512

    P = dict(
        norm1=norm1.reshape(depth, 1, D_MODEL), norm2=norm2.reshape(depth, 1, D_MODEL),
        w_even_main=jnp.pad(w_in_even, ((0, 0), (0, 0), (0, LANES - N_GATES))).astype(BF16),
        b_gate=jnp.pad(b_gate_even.reshape(n_even, 1, N_GATES), ((0, 0), (0, 0), (0, LANES - N_GATES))),
        gq_even=qk_gain_even[:, 0].reshape(n_even, 1, LANES), gk_even=qk_gain_even[:, 1].reshape(n_even, 1, LANES),
        lam_even=lam_even, da_norm=da_norm_even.reshape(n_even, 1, LANES),
        ml_norm=ml_norm_even.reshape(n_even, 1, ML_DIM),
        w_out_even=w_out_even.astype(BF16), w_in_odd=w_in_odd.astype(BF16),
        gq_odd=jnp.tile(qk_gain_odd[:, 0], (1, 2)).reshape(n_odd, 1, LANES),
        gk_odd=jnp.tile(qk_gain_odd[:, 1], (1, 2)).reshape(n_odd, 1, LANES),
        w_out_odd=w_out_odd.astype(BF16), w_ffn_in=w_ffn_in.astype(BF16), w_ffn_out=w_ffn_out.astype(BF16))

    cond8 = jnp.concatenate([c_ctx[None, :], c, jnp.zeros((8 - 1 - dbatch, D_MODEL), F32)], axis=0)
    mod = _modulation(cond8, w_mod, b_mod)
    rope_tabs = _rope_tables(dseq)

    x = x_prompt.reshape(batch * seq, D_MODEL)
    new_even = new_odd = None
    for li in range(depth):
        mods = mod[li, 0:1].reshape(1, 1, 6 * D_MODEL)
        if li % 2 == 0:
            x, new_even = _trunk(x, mods, li, P, batch, None, None, tm, new_even, n_even)
        else:
            x, new_odd = _trunk(x, mods, li, P, batch, None, None, tm, new_odd, n_odd)
    y_prompt = x.reshape(batch, seq, D_MODEL)

    x = x_sample.reshape(dbatch * dseq, D_MODEL)
    for li in range(depth):
        j = li // 2
        if li % 2 == 0:
            ctx = (cache_da_k[:, j].reshape(dbatch * past, SEG).astype(BF16),
                   _with_ones(cache_da_v[:, j].reshape(dbatch * past, SEG)),
                   (state_mlstm_C[:, j], state_mlstm_n[:, j][:, :, :, None, :],
                    jnp.broadcast_to(state_mlstm_m[:, j][:, :, :, None, None], (dbatch, 2, ML_HEADS, 1, ML_DIM))))
        else:
            ctx = (cache_gqa_k[:, j].reshape(dbatch * past, GQA_KV_HEADS * HEAD_DIM).astype(BF16),
                   _with_ones(cache_gqa_v[:, j].reshape(dbatch * past, GQA_KV_HEADS * HEAD_DIM)))
        x, _ = _trunk(x, mod[li, 1:1 + dbatch].reshape(dbatch, 1, 6 * D_MODEL), li, P, dbatch, rope_tabs, ctx, tm)
    y_sample = x.reshape(dbatch, dseq, D_MODEL)

    new_da_k = new_even[0].reshape(batch, n_even, seq, DA_HEADS, 2, HEAD_DIM)
    new_da_v = new_even[1].reshape(batch, n_even, seq, DA_HEADS, 2 * HEAD_DIM)
    new_c = new_even[2]
    new_n = new_even[3][:, :, :, :, 0, :]
    new_m = new_even[4][:, :, :, :, 0, 0]
    new_gqa_k = new_odd[0].reshape(batch, n_odd, seq, GQA_KV_HEADS, HEAD_DIM)
    new_gqa_v = new_odd[1].reshape(batch, n_odd, seq, GQA_KV_HEADS, HEAD_DIM)
    return (y_prompt, y_sample, new_da_k, new_da_v, new_c, new_n, new_m, new_gqa_k, new_gqa_v)
```

```python
import functools
import math

import jax
import jax.numpy as jnp
from jax import lax
from jax.experimental import pallas as pl
from jax.experimental.pallas import tpu as pltpu

F32 = jnp.float32
BF16 = jnp.bfloat16
HIGHEST = lax.Precision.HIGHEST

D_MODEL = 1024
GRID_W = 64
HEAD_DIM = 64
ROPE_PAIRS = HEAD_DIM // 4
ROPE_THETA = 10000.0
EPS = 1e-6
DA_HEADS = 4
ML_HEADS = 4
ML_DIM = 128
N_GATES = 2 * 2 * ML_HEADS
GQA_Q_HEADS = 16
GQA_KV_HEADS = 4
SEG = 512
EVEN_MAIN = 7 * SEG
D_FF = -(-8 * D_MODEL // (3 * 256)) * 256

LANES = 128
VMEM_LIMIT = 56 * 2 ** 20

ATTN_TQ = 256
ATTN_TK = 256
ATTN_LAG = 3
Q_SCALE = math.log2(math.e) * HEAD_DIM ** -0.5
ML_CHUNK = 128
FF_CHUNK = 256
ML_PAD = 8


def _params(sem, vmem=VMEM_LIMIT):
    return pltpu.CompilerParams(dimension_semantics=sem, vmem_limit_bytes=vmem)


def _round_robin(generators):
    results = [None] * len(generators)
    live = list(enumerate(generators))
    while live:
        still = []
        for idx, gen in live:
            try:
                next(gen)
                still.append((idx, gen))
            except StopIteration as stop:
                results[idx] = stop.value
        live = still
    return results


def _rms_rows(x, g):
    return x * lax.rsqrt(jnp.mean(x * x, axis=-1, keepdims=True) + EPS) * g


def _lane_lo():
    return lax.broadcasted_iota(jnp.int32, (1, LANES), 1) < HEAD_DIM


def _half_rms(x, gain, lo):
    x2 = x * x
    s_lo = jnp.sum(jnp.where(lo, x2, 0.0), axis=-1, keepdims=True)
    s_hi = jnp.sum(jnp.where(lo, 0.0, x2), axis=-1, keepdims=True)
    r = jnp.where(lo, lax.rsqrt(s_lo * (1.0 / HEAD_DIM) + EPS), lax.rsqrt(s_hi * (1.0 / HEAD_DIM) + EPS))
    return x * r * gain


def _half_indicator():
    r = lax.broadcasted_iota(jnp.int32, (LANES, LANES), 0) // HEAD_DIM
    c = lax.broadcasted_iota(jnp.int32, (LANES, LANES), 1) // HEAD_DIM
    return jnp.where(r == c, 1.0, 0.0).astype(BF16)


def _half_rms_mxu(x, gain, ind):
    ss = jnp.dot((x * x).astype(BF16), ind, preferred_element_type=F32)
    return x * lax.rsqrt(ss * (1.0 / HEAD_DIM) + EPS) * gain


def _rope(y, cos, sin_a, sin_b):
    return y * cos + pltpu.roll(y, LANES - ROPE_PAIRS, 1) * sin_a + pltpu.roll(y, ROPE_PAIRS, 1) * sin_b


def _sigmoid(x):
    return 1.0 / (1.0 + jnp.exp(-x))


def _store_cache(ref, cols, y, seq):
    for b in range(y.shape[0] // seq):
        ref[b, 0, :, cols] = y[b * seq:(b + 1) * seq]


def _cache_outputs(cache, rows, width, tm):
    seq, layer, n_layers, prev = cache
    shape = jax.ShapeDtypeStruct((rows // seq, n_layers, seq, width), F32)
    spec = pl.BlockSpec((tm // seq, 1, seq, width), lambda i: (i, layer, 0, 0))
    return shape, spec


def _store_with_ones(v_o, v):
    ones = jnp.ones((v.shape[0], LANES), BF16)
    for g in range(v.shape[1] // LANES):
        v_o[:, 2 * g * LANES:(2 * g + 1) * LANES] = v[:, g * LANES:(g + 1) * LANES].astype(BF16)
        v_o[:, (2 * g + 1) * LANES:(2 * g + 2) * LANES] = ones


def _mod_kernel(cond_ref, w_ref, b_ref, o_ref):
    c = cond_ref[...]
    a = c * _sigmoid(c)
    o_ref[0] = jnp.dot(a, w_ref[0], preferred_element_type=F32, precision=HIGHEST) + b_ref[0]


def _modulation(cond8, w_mod, b_mod):
    depth, _, n = w_mod.shape
    tn = 1024
    return pl.pallas_call(
        _mod_kernel,
        grid=(depth, n // tn),
        in_specs=[pl.BlockSpec((8, D_MODEL), lambda l, j: (0, 0)),
                  pl.BlockSpec((1, D_MODEL, tn), lambda l, j: (l, 0, j)),
                  pl.BlockSpec((1, 1, tn), lambda l, j: (l, 0, j))],
        out_specs=pl.BlockSpec((1, 8, tn), lambda l, j: (l, 0, j)),
        out_shape=jax.ShapeDtypeStruct((depth, 8, n), F32),
        compiler_params=_params(("arbitrary", "arbitrary")), name="modulation",
    )(cond8, w_mod, b_mod.reshape(depth, 1, n))


def _modulated(x_ref, mod_ref, norm_ref, which):
    sh = mod_ref[0, :, (3 * which) * D_MODEL:(3 * which + 1) * D_MODEL]
    sc = mod_ref[0, :, (3 * which + 1) * D_MODEL:(3 * which + 2) * D_MODEL]
    return _rms_rows(x_ref[...], norm_ref[...]) * (1.0 + sc) + sh


def _even_inproj_kernel(*refs, rope, cache, n_prev):
    it = iter(refs)
    x_ref, mod_ref, n1_ref, w_ref, bg_ref, gq_ref, gk_ref = [next(it) for _ in range(7)]
    if rope:
        cos_ref, sa_ref, sb_ref = [next(it) for _ in range(3)]
    for _ in range(n_prev):
        next(it)
    q_o, k_o, v_o, mq_o, mk_o, mv_o, mo_o, gt_o = [next(it) for _ in range(8)]
    if cache:
        kf_o, vf_o = [next(it) for _ in range(2)]

    hb = _modulated(x_ref, mod_ref, n1_ref, 0).astype(BF16)
    lo = _lane_lo()

    def seg(i):
        return jnp.dot(hb, w_ref[:, i * SEG:(i + 1) * SEG], preferred_element_type=F32)

    def qk_norm(p, gain_ref, hh):
        y = _half_rms(p[:, hh * LANES:(hh + 1) * LANES], gain_ref[...], lo)
        if rope:
            y = _rope(y, cos_ref[...], sa_ref[...], sb_ref[...])
        return y

    p = seg(0)
    for hh in range(DA_HEADS):
        q_o[:, hh * LANES:(hh + 1) * LANES] = (qk_norm(p, gq_ref, hh) * Q_SCALE).astype(BF16)
    p = seg(1)
    for hh in range(DA_HEADS):
        sl = slice(hh * LANES, (hh + 1) * LANES)
        y = qk_norm(p, gk_ref, hh)
        if cache:
            _store_cache(kf_o, sl, y, cache)
        k_o[:, sl] = y.astype(BF16)
    p = seg(2)
    if cache:
        v_o[...] = p.astype(BF16)
        _store_cache(vf_o, slice(None), p, cache)
    else:
        _store_with_ones(v_o, p)
    mq_o[...] = seg(3).astype(BF16)
    mk_o[...] = (seg(4) * (ML_DIM ** -0.5)).astype(BF16)
    mv_o[...] = seg(5).astype(BF16)
    mo_o[...] = _sigmoid(seg(6)).astype(BF16)
    pg = jnp.dot(hb, w_ref[:, EVEN_MAIN:EVEN_MAIN + LANES], preferred_element_type=F32) + bg_ref[...]
    gt_o[...] = pg.T[:N_GATES, :]


def _even_inproj(x, mod, norm1, w_main, b_gate, gain_q, gain_k, rope_tabs, rows_per_batch, cache, tm):
    r = x.shape[0]
    nt = r // tm
    tpb = rows_per_batch // tm
    row = lambda i: (i, 0)
    const = lambda i: (0, 0)
    in_specs = [pl.BlockSpec((tm, D_MODEL), row),
                pl.BlockSpec((1, 1, 6 * D_MODEL), lambda i: (i // tpb, 0, 0)),
                pl.BlockSpec((1, D_MODEL), const),
                pl.BlockSpec((D_MODEL, EVEN_MAIN + LANES), const),
                pl.BlockSpec((1, LANES), const),
                pl.BlockSpec((1, LANES), const),
                pl.BlockSpec((1, LANES), const)]
    args = [x, mod, norm1, w_main, b_gate, gain_q, gain_k]
    if rope_tabs is not None:
        in_specs += [pl.BlockSpec((tm, LANES), lambda i: (i % tpb, 0))] * 3
        args += list(rope_tabs)
    out_shape = [jax.ShapeDtypeStruct((r, SEG), BF16)] * 7
    out_specs = [pl.BlockSpec((tm, SEG), row)] * 7
    if not cache:
        out_shape[2] = jax.ShapeDtypeStruct((r, 2 * SEG), BF16)
        out_specs[2] = pl.BlockSpec((tm, 2 * SEG), row)
    out_shape += [jax.ShapeDtypeStruct((N_GATES, r), F32)]
    out_specs += [pl.BlockSpec((N_GATES, tm), lambda i: (0, i))]
    aliases = {}
    if cache:
        shape, spec = _cache_outputs(cache, r, SEG, tm)
        for prev in cache[3] or ():
            aliases[len(args)] = len(out_shape)
            in_specs.append(pl.BlockSpec(memory_space=pl.ANY))
            args.append(prev)
            out_shape.append(shape)
            out_specs.append(spec)
        if not cache[3]:
            out_shape += [shape] * 2
            out_specs += [spec] * 2
    return pl.pallas_call(
        functools.partial(_even_inproj_kernel, rope=rope_tabs is not None, cache=cache[0] if cache else None,
                          n_prev=len(aliases)),
        grid=(nt,), in_specs=in_specs, out_specs=out_specs, out_shape=out_shape, input_output_aliases=aliases,
        compiler_params=_params(("arbitrary",)), name="even_inproj_ctx" if cache else "even_inproj_dec",
    )(*args)


def _odd_inproj_kernel(*refs, rope, cache, n_prev):
    it = iter(refs)
    x_ref, mod_ref, n1_ref, w_ref, gq_ref, gk_ref = [next(it) for _ in range(6)]
    if rope:
        cos_ref, sa_ref, sb_ref = [next(it) for _ in range(3)]
    for _ in range(n_prev):
        next(it)
    q_o, k_o, v_o = [next(it) for _ in range(3)]
    if cache:
        kf_o, vf_o = [next(it) for _ in range(2)]

    hb = _modulated(x_ref, mod_ref, n1_ref, 0).astype(BF16)
    ind = _half_indicator() if rope else None
    lo = _lane_lo()
    nq = GQA_Q_HEADS * HEAD_DIM
    nkv = GQA_KV_HEADS * HEAD_DIM

    def qk_norm(t, gain_ref):
        y = _half_rms_mxu(t, gain_ref[...], ind) if rope else _half_rms(t, gain_ref[...], lo)
        if rope:
            y = _rope(y, cos_ref[...], sa_ref[...], sb_ref[...])
        return y

    for c in range(nq // SEG):
        p = jnp.dot(hb, w_ref[:, c * SEG:(c + 1) * SEG], preferred_element_type=F32)
        for hh in range(SEG // LANES):
            sl = slice(c * SEG + hh * LANES, c * SEG + (hh + 1) * LANES)
            q_o[:, sl] = (qk_norm(p[:, hh * LANES:(hh + 1) * LANES], gq_ref) * Q_SCALE).astype(BF16)
    p = jnp.dot(hb, w_ref[:, nq:nq + 2 * nkv], preferred_element_type=F32)
    for hh in range(nkv // LANES):
        sl = slice(hh * LANES, (hh + 1) * LANES)
        y = qk_norm(p[:, sl], gk_ref)
        if cache:
            _store_cache(kf_o, sl, y, cache)
        k_o[:, sl] = y.astype(BF16)
    v = p[:, nkv:2 * nkv]
    if cache:
        v_o[...] = v.astype(BF16)
        _store_cache(vf_o, slice(None), v, cache)
    else:
        _store_with_ones(v_o, v)


def _odd_inproj(x, mod, norm1, w, gain_q, gain_k, rope_tabs, rows_per_batch, cache, tm):
    r = x.shape[0]
    nt = r // tm
    tpb = rows_per_batch // tm
    nq = GQA_Q_HEADS * HEAD_DIM
    nkv = GQA_KV_HEADS * HEAD_DIM
    row = lambda i: (i, 0)
    const = lambda i: (0, 0)
    in_specs = [pl.BlockSpec((tm, D_MODEL), row),
                pl.BlockSpec((1, 1, 6 * D_MODEL), lambda i: (i // tpb, 0, 0)),
                pl.BlockSpec((1, D_MODEL), const),
                pl.BlockSpec((D_MODEL, nq + 2 * nkv), const),
                pl.BlockSpec((1, LANES), const),
                pl.BlockSpec((1, LANES), const)]
    args = [x, mod, norm1, w, gain_q, gain_k]
    if rope_tabs is not None:
        in_specs += [pl.BlockSpec((tm, LANES), lambda i: (i % tpb, 0))] * 3
        args += list(rope_tabs)
    out_shape = [jax.ShapeDtypeStruct((r, nq), BF16), jax.ShapeDtypeStruct((r, nkv), BF16),
                 jax.ShapeDtypeStruct((r, nkv), BF16)]
    out_specs = [pl.BlockSpec((tm, nq), row), pl.BlockSpec((tm, nkv), row), pl.BlockSpec((tm, nkv), row)]
    if not cache:
        out_shape[2] = jax.ShapeDtypeStruct((r, 2 * nkv), BF16)
        out_specs[2] = pl.BlockSpec((tm, 2 * nkv), row)
    aliases = {}
    if cache:
        shape, spec = _cache_outputs(cache, r, nkv, tm)
        for prev in cache[3] or ():
            aliases[len(args)] = len(out_shape)
            in_specs.append(pl.BlockSpec(memory_space=pl.ANY))
            args.append(prev)
            out_shape.append(shape)
            out_specs.append(spec)
        if not cache[3]:
            out_shape += [shape] * 2
            out_specs += [spec] * 2
    return pl.pallas_call(
        functools.partial(_odd_inproj_kernel, rope=rope_tabs is not None, cache=cache[0] if cache else None,
                          n_prev=len(aliases)),
        grid=(nt,), in_specs=in_specs, out_specs=out_specs, out_shape=out_shape, input_output_aliases=aliases,
        compiler_params=_params(("arbitrary",)), name="odd_inproj_ctx" if cache else "odd_inproj_dec",
    )(*args)


_NT = (((1,), (1,)), ((), ()))


def _attend(q, k, v):
    s = lax.dot_general(q, k, _NT, preferred_element_type=F32)
    yield
    p = jnp.exp2(s - jnp.max(s, axis=-1, keepdims=True))
    l = jnp.sum(p, axis=-1, keepdims=True)
    yield
    acc = jnp.dot(p.astype(BF16), v, preferred_element_type=F32)
    yield
    return acc / l


def _lam_full(lam_ref, lam_init):
    lam = lam_ref[...]
    return (jnp.exp(jnp.sum(lam[0:1] * lam[1:2], axis=-1, keepdims=True))
            - jnp.exp(jnp.sum(lam[2:3] * lam[3:4], axis=-1, keepdims=True)) + lam_init)


def _da_ctx_kernel(q_ref, k_ref, v_ref, lam_ref, gn_ref, o_ref, *, lam_init):
    lo = _lane_lo()
    lam_full = _lam_full(lam_ref, lam_init)
    chains = []
    for h in range(DA_HEADS):
        sl = slice(h * LANES, (h + 1) * LANES)
        q, k, v = q_ref[:, sl], k_ref[:, sl], v_ref[:, sl]
        zero = jnp.zeros_like(q)
        chains += [_attend(jnp.where(lo, q, zero), k, v), _attend(jnp.where(lo, zero, q), k, v)]
    maps = _round_robin(chains)
    for h in range(DA_HEADS):
        o = maps[2 * h] - lam_full * maps[2 * h + 1]
        o_ref[:, h * LANES:(h + 1) * LANES] = (_rms_rows(o, gn_ref[...]) * (1.0 - lam_init)).astype(BF16)


def _da_ctx_attention(q, k, v, lam, gain, batch, lam_init):
    r = q.shape[0]
    s = r // batch
    blk = pl.BlockSpec((s, SEG), lambda b: (b, 0))
    const = lambda b: (0, 0)
    return pl.pallas_call(
        functools.partial(_da_ctx_kernel, lam_init=lam_init), grid=(batch,),
        in_specs=[blk, blk, blk, pl.BlockSpec(lam.shape, const), pl.BlockSpec((1, LANES), const)],
        out_specs=blk, out_shape=jax.ShapeDtypeStruct((r, SEG), BF16),
        compiler_params=_params(("arbitrary",)), name="da_attn_ctx",
    )(q, k, v, lam, gain)


def _gqa_ctx_kernel(q_ref, k_ref, v_ref, o_ref):
    lo = _lane_lo()
    group_pairs = GQA_Q_HEADS // GQA_KV_HEADS // 2
    chains, keeps = [], []
    for j in range(GQA_KV_HEADS // 2):
        k, v = k_ref[:, j * LANES:(j + 1) * LANES], v_ref[:, j * LANES:(j + 1) * LANES]
        for half in range(2):
            keep = lo if half == 0 else jnp.logical_not(lo)
            for pr in range(group_pairs):
                col = ((j * 2 + half) * group_pairs + pr) * LANES
                qp = q_ref[:, col:col + LANES].astype(F32)
                qr = pltpu.roll(qp, HEAD_DIM, 1)
                chains.append(_attend(jnp.where(keep, jnp.where(lo, qp, qr), 0.0).astype(BF16), k, v))
                chains.append(_attend(jnp.where(keep, jnp.where(lo, qr, qp), 0.0).astype(BF16), k, v))
                keeps.append(keep)
    heads = _round_robin(chains)
    for pair, keep in enumerate(keeps):
        oa, ob = heads[2 * pair], heads[2 * pair + 1]
        first = jnp.where(keep, oa, pltpu.roll(oa, HEAD_DIM, 1))
        second = jnp.where(keep, ob, pltpu.roll(ob, HEAD_DIM, 1))
        o_ref[:, pair * LANES:(pair + 1) * LANES] = jnp.where(lo, first, second).astype(BF16)


def _gqa_ctx_attention(q, k, v, batch):
    r = q.shape[0]
    s = r // batch
    row = lambda b: (b, 0)
    return pl.pallas_call(
        _gqa_ctx_kernel, grid=(batch,),
        in_specs=[pl.BlockSpec((s, q.shape[1]), row), pl.BlockSpec((s, k.shape[1]), row),
                  pl.BlockSpec((s, v.shape[1]), row)],
        out_specs=pl.BlockSpec((s, q.shape[1]), row), out_shape=jax.ShapeDtypeStruct(q.shape, BF16),
        compiler_params=_params(("arbitrary",)), name="gqa_attn_ctx",
    )(q, k, v)


def _key_chunks(ref, cache_ref):
    chunks, col = [], 0
    for r in (ref, cache_ref):
        for c in range(r.shape[0] // ATTN_TK):
            chunks.append((r, c * ATTN_TK, col))
            col += ATTN_TK
    return chunks


def _score_head(q, k_ref, ck_ref, s_w, m_w, a):
    mx = None
    for ref, off, col in _key_chunks(k_ref, ck_ref):
        s = lax.dot_general(q, ref[off:off + ATTN_TK, :], _NT, preferred_element_type=F32)
        s_w[a, :, col:col + ATTN_TK] = s
        tile_max = functools.reduce(jnp.maximum, [s[:, i * LANES:(i + 1) * LANES] for i in range(ATTN_TK // LANES)])
        mx = tile_max if mx is None else jnp.maximum(mx, tile_max)
        yield
    m_w[a] = jnp.max(mx, axis=-1, keepdims=True)


def _finish_head(v_ref, cv_ref, s_r, m_r, a):
    m = m_r[a]
    acc = None
    for ref, off, col in _key_chunks(v_ref, cv_ref):
        p = jnp.exp2(s_r[a, :, col:col + ATTN_TK] - m).astype(BF16)
        part = jnp.dot(p, ref[off:off + ATTN_TK, :], preferred_element_type=F32)
        acc = part if acc is None else acc + part
        yield
    return acc[:, 0:LANES] / acc[:, LANES:LANES + 1]


def _score_and_finish(qs, k_ref, ck_ref, s_w, m_w, v_ref, cv_ref, s_r, m_r):
    def lagged(gen):
        for _ in range(ATTN_LAG):
            yield
        return (yield from gen)

    stages = [_finish_head(v_ref, cv_ref, s_r, m_r, a) for a in range(2)]
    stages += [lagged(_score_head(qs[a], k_ref, ck_ref, s_w, m_w, a)) for a in range(2)]
    results = _round_robin(stages)
    return results[0], results[1]


def _by_parity(t, body, sa, ma, sb, mb):
    @pl.when(t == 0)
    def _():
        sb[...] = jnp.zeros(sb.shape, F32)
        mb[...] = jnp.zeros(mb.shape, F32)

    @pl.when(t % 2 == 0)
    def _():
        body(sa, ma, sb, mb)

    @pl.when(t % 2 == 1)
    def _():
        body(sb, mb, sa, ma)


def _da_pipe_kernel(q_ref, k_ref, ck_ref, v_ref, cv_ref, lam_ref, gn_ref, o_ref, sa, ma, sb, mb, *, lam_init):
    lo = _lane_lo()

    def body(s_w, m_w, s_r, m_r):
        q = q_ref[...]
        zero = jnp.zeros_like(q)
        a1, a2 = _score_and_finish([jnp.where(lo, q, zero), jnp.where(lo, zero, q)], k_ref, ck_ref, s_w, m_w,
                                   v_ref, cv_ref, s_r, m_r)
        o = a1 - _lam_full(lam_ref, lam_init) * a2
        o_ref[...] = (_rms_rows(o, gn_ref[...]) * (1.0 - lam_init)).astype(BF16)

    _by_parity(pl.program_id(0), body, sa, ma, sb, mb)


def _gqa_pipe_kernel(q_ref, k_ref, ck_ref, v_ref, cv_ref, o_ref, sa, ma, sb, mb, *, n_items):
    t = pl.program_id(0)
    lo = _lane_lo()
    q_pairs = GQA_Q_HEADS // GQA_KV_HEADS

    def keep_of(item):
        half = (item % q_pairs) // (q_pairs // 2)
        return jnp.where(lo, 0, 1) == half

    def body(s_w, m_w, s_r, m_r):
        keep = keep_of(jnp.minimum(t, n_items - 1))
        qp = q_ref[...].astype(F32)
        qr = pltpu.roll(qp, HEAD_DIM, 1)
        qa = jnp.where(keep, jnp.where(lo, qp, qr), 0.0).astype(BF16)
        qb = jnp.where(keep, jnp.where(lo, qr, qp), 0.0).astype(BF16)
        oa, ob = _score_and_finish([qa, qb], k_ref, ck_ref, s_w, m_w, v_ref, cv_ref, s_r, m_r)
        keep = keep_of(jnp.maximum(t - 1, 0))
        first = jnp.where(keep, oa, pltpu.roll(oa, HEAD_DIM, 1))
        second = jnp.where(keep, ob, pltpu.roll(ob, HEAD_DIM, 1))
        o_ref[...] = jnp.where(lo, first, second).astype(BF16)

    _by_parity(t, body, sa, ma, sb, mb)


def _pipe_attention(kind, q, k, v1, ck, cv1, batch, tq, extra=(), lam_init=None):
    r = q.shape[0]
    s = r // batch
    past = ck.shape[0] // batch
    nq = s // tq
    groups = k.shape[1] // LANES
    per_q = q.shape[1] // LANES // groups
    per_g = nq * per_q
    n_items = batch * groups * per_g

    def split(item):
        b, g, rest = item // (groups * per_g), (item // per_g) % groups, item % per_g
        return b, g, rest // per_q, rest % per_q

    def q_index(item):
        b, g, i, p = split(item)
        return (b * nq + i, g * per_q + p)

    def kv_index(item):
        b, g, _, _ = split(item)
        return (b, g)

    cur = lambda t: jnp.minimum(t, n_items - 1)
    prev = lambda t: jnp.maximum(t - 1, 0)
    in_specs = [pl.BlockSpec((tq, LANES), lambda t: q_index(cur(t))),
                pl.BlockSpec((s, LANES), lambda t: kv_index(cur(t))),
                pl.BlockSpec((past, LANES), lambda t: kv_index(cur(t))),
                pl.BlockSpec((s, 2 * LANES), lambda t: kv_index(prev(t))),
                pl.BlockSpec((past, 2 * LANES), lambda t: kv_index(prev(t)))]
    in_specs += [pl.BlockSpec(e.shape, lambda t: (0, 0)) for e in extra]
    if kind == "da":
        body = functools.partial(_da_pipe_kernel, lam_init=lam_init)
    else:
        body = functools.partial(_gqa_pipe_kernel, n_items=n_items)
    scratch = [pltpu.VMEM((2, tq, s + past), F32), pltpu.VMEM((2, tq, 1), F32)] * 2
    return pl.pallas_call(
        body, grid=(n_items + 1,), in_specs=in_specs,
        out_specs=pl.BlockSpec((tq, LANES), lambda t: q_index(prev(t))),
        out_shape=jax.ShapeDtypeStruct(q.shape, BF16),
        scratch_shapes=scratch,
        compiler_params=_params(("arbitrary",)),
        name=kind + "_attn_dec",
    )(q, k, ck, v1, cv1, *extra)


def _log_sigmoid(x):
    return jnp.minimum(x, 0.0) - jnp.log1p(jnp.exp(-jnp.abs(x)))


def _mlstm_chunk(q, k, v, gt, c_s, n_s, m_s, head, direction):
    L, D = ML_CHUNK, ML_DIM
    i_idx = direction * 2 * ML_HEADS + head
    f_idx = i_idx + ML_HEADS
    i_row, f_row = gt[i_idx:i_idx + 1, :], gt[f_idx:f_idx + 1, :]
    lf_row = _log_sigmoid(f_row)
    r_i = lax.broadcasted_iota(jnp.int32, (L, L), 0)
    c_i = lax.broadcasted_iota(jnp.int32, (L, L), 1)
    vis = (r_i <= c_i) if direction == 0 else (r_i >= c_i)
    between = (c_i > r_i) if direction == 0 else (c_i < r_i)
    lhs_terms, lf_rem, i_rem = [], lf_row, i_row
    for _ in range(3):
        lf_t, i_t = lf_rem.astype(BF16), i_rem.astype(BF16)
        lf_rem, i_rem = lf_rem - lf_t.astype(F32), i_rem - i_t.astype(F32)
        lf_f, i_f = lf_t.astype(F32), i_t.astype(F32)
        lhs_terms.append(jnp.concatenate([
            jnp.concatenate([jnp.where(between, lf_f, 0.0), jnp.where(r_i == c_i, i_f, 0.0)], axis=1),
            jnp.concatenate([jnp.broadcast_to(lf_f, (ML_PAD, L)), jnp.zeros((ML_PAD, L), F32)], axis=1)],
            axis=0).astype(BF16))
    rhs = jnp.concatenate([jnp.where(vis, 1.0, 0.0), jnp.ones((L, L), F32)], axis=0).astype(BF16)
    lhs, rhs = jnp.concatenate(lhs_terms, axis=1), jnp.concatenate([rhs] * 3, axis=0)
    yield
    d_all = jnp.dot(lhs, rhs, preferred_element_type=F32)
    yield
    b_row = d_all[L:L + 1, :]
    last = slice(L - 1, L) if direction == 0 else slice(0, 1)
    b_last = b_row[:, last]
    m_prev = m_s[:, 0:1]
    dmat = jnp.where(vis, d_all[0:L, :], -jnp.inf)
    inter = b_row + m_prev
    m_t = jnp.maximum(inter, jnp.max(dmat, axis=0, keepdims=True))
    w_intra = jnp.exp(dmat - m_t)
    w_inter = jnp.exp(inter - m_t)
    yield
    a_t = (lax.dot_general(k, q, _NT, preferred_element_type=F32) * w_intra).astype(BF16)
    v_t = v.astype(F32).T
    v1 = jnp.concatenate([v_t, jnp.ones((ML_PAD, L), F32)], axis=0).astype(BF16)
    state = jnp.concatenate([c_s[...], jnp.broadcast_to(n_s[...], (ML_PAD, D))], axis=0)
    yield
    tot = (jnp.dot(v1, a_t, preferred_element_type=F32)
           + w_inter * lax.dot_general(state.astype(BF16), q, _NT, preferred_element_type=F32))
    yield
    h_t = tot[0:D, :] / jnp.maximum(jnp.abs(tot[D:D + 1, :]), jnp.exp(-m_t))
    m_new = m_t[:, last]
    w_s = jnp.exp(b_last - b_row + i_row - m_new)
    decay = jnp.exp(b_last + m_prev - m_new)
    vw = jnp.concatenate([v_t * w_s, jnp.broadcast_to(w_s, (ML_PAD, L))], axis=0).astype(BF16)
    yield
    state = decay * state + jnp.dot(vw, k, preferred_element_type=F32)
    return h_t.T, state[0:D, :], state[D:D + 1, :], jnp.broadcast_to(m_new, (1, D))


def _mlstm_kernel(*refs, has_init, n_prev):
    it = iter(refs)
    sides = [[next(it) for _ in range(4)] for _ in range(2)]
    if has_init:
        c0, n0, m0 = [next(it) for _ in range(3)]
    for _ in range(n_prev):
        next(it)
    h_os = [next(it) for _ in range(2)]
    c_o, n_o, m_o = [next(it) for _ in range(3)]
    c_s, n_s, m_s = [next(it) for _ in range(3)]
    chunk = pl.program_id(1)

    @pl.when(chunk == 0)
    def _():
        if has_init:
            c_s[...] = c0[0]
            n_s[...] = n0[0]
            m_s[...] = m0[0]
        else:
            for ref in (c_s, n_s, m_s):
                ref[...] = jnp.zeros(ref.shape, F32)

    chains = []
    for d, (q_ref, k_ref, v_ref, gt_ref) in enumerate(sides):
        gt = gt_ref[...]
        for h in range(ML_HEADS):
            cols = slice(h * ML_DIM, (h + 1) * ML_DIM)
            chains.append(_mlstm_chunk(q_ref[:, cols], k_ref[:, cols], v_ref[:, cols], gt,
                                       c_s.at[d, h], n_s.at[d, h], m_s.at[d, h], h, d))
    results = _round_robin(chains)
    for d, h_o in enumerate(h_os):
        h_o[...] = jnp.concatenate([r[0] for r in results[d * ML_HEADS:(d + 1) * ML_HEADS]], axis=1)
    c_s[...] = jnp.stack([r[1] for r in results]).reshape(c_s.shape)
    n_s[...] = jnp.stack([r[2] for r in results]).reshape(n_s.shape)
    m_s[...] = jnp.stack([r[3] for r in results]).reshape(m_s.shape)

    @pl.when(chunk == pl.num_programs(1) - 1)
    def _():
        c_o[0, 0] = c_s[...]
        n_o[0, 0] = n_s[...]
        m_o[0, 0] = m_s[...]


def _mlstm(q, k, v, gt, init, batch, layer=0, n_layers=1, prev=None):
    r = q.shape[0]
    L = ML_CHUNK
    nc = r // batch // L
    width = ML_HEADS * ML_DIM
    fwd = lambda b, c: b * nc + c
    bwd = lambda b, c: b * nc + nc - 1 - c

    def side(pos):
        return ([pl.BlockSpec((L, width), lambda b, c: (pos(b, c), 0))] * 3
                + [pl.BlockSpec((N_GATES, L), lambda b, c: (0, pos(b, c)))])

    in_specs = side(fwd) + side(bwd)
    args = [q, k, v, gt] * 2
    state_shapes = [(2, ML_HEADS, ML_DIM, ML_DIM), (2, ML_HEADS, 1, ML_DIM), (2, ML_HEADS, 1, ML_DIM)]
    state_specs = [pl.BlockSpec((1,) + shp, lambda b, c: (b, 0, 0, 0, 0)) for shp in state_shapes]
    if init is not None:
        in_specs += state_specs
        args += list(init)
    aliases = {}
    for j, arr in enumerate(prev or ()):
        aliases[len(args)] = 2 + j
        in_specs.append(pl.BlockSpec(memory_space=pl.ANY))
        args.append(arr)
    out_shape = ([jax.ShapeDtypeStruct((r, width), F32)] * 2
                 + [jax.ShapeDtypeStruct((batch, n_layers) + shp, F32) for shp in state_shapes])
    out_specs = [pl.BlockSpec((L, width), lambda b, c: (fwd(b, c), 0)),
                 pl.BlockSpec((L, width), lambda b, c: (bwd(b, c), 0))]
    out_specs += [pl.BlockSpec((1, 1) + shp, lambda b, c: (b, layer, 0, 0, 0, 0)) for shp in state_shapes]
    return pl.pallas_call(
        functools.partial(_mlstm_kernel, has_init=init is not None, n_prev=len(aliases)),
        grid=(batch, nc), in_specs=in_specs, out_specs=out_specs, out_shape=out_shape, input_output_aliases=aliases,
        scratch_shapes=[pltpu.VMEM(shp, F32) for shp in state_shapes],
        compiler_params=_params(("arbitrary", "arbitrary")),
        name="mlstm_dec" if init is not None else "mlstm_ctx",
    )(*args)


def _gate(mod_ref, which):
    return mod_ref[0, :, (3 * which + 2) * D_MODEL:(3 * which + 3) * D_MODEL]


def _even_outproj_kernel(da_ref, hf_ref, hb_ref, mo_ref, gn_ref, w_ref, x_ref, mod_ref, o_ref):
    ys = []
    for hh in range(ML_HEADS):
        sl = slice(hh * ML_DIM, (hh + 1) * ML_DIM)
        y = _rms_rows(hf_ref[:, sl] + hb_ref[:, sl], gn_ref[...]) * mo_ref[:, sl].astype(F32)
        ys.append(y.astype(BF16))
    mix = jnp.concatenate([da_ref[...]] + ys, axis=1)
    acc = jnp.dot(mix, w_ref[...], preferred_element_type=F32)
    o_ref[...] = x_ref[...] + _gate(mod_ref, 0) * acc


def _even_outproj(da, hf, hb, mo, gain, w, x, mod, rows_per_batch, tm):
    r = x.shape[0]
    tpb = rows_per_batch // tm
    row = lambda i: (i, 0)
    const = lambda i: (0, 0)
    return pl.pallas_call(
        _even_outproj_kernel, grid=(r // tm,),
        in_specs=[pl.BlockSpec((tm, SEG), row)] * 4 + [
            pl.BlockSpec((1, ML_DIM), const), pl.BlockSpec(w.shape, const),
            pl.BlockSpec((tm, D_MODEL), row), pl.BlockSpec((1, 1, 6 * D_MODEL), lambda i: (i // tpb, 0, 0))],
        out_specs=pl.BlockSpec((tm, D_MODEL), row),
        out_shape=jax.ShapeDtypeStruct(x.shape, F32),
        compiler_params=_params(("arbitrary",)), name="even_outproj",
    )(da, hf, hb, mo, gain, w, x, mod)


def _odd_outproj_kernel(a_ref, w_ref, x_ref, mod_ref, o_ref):
    acc = jnp.dot(a_ref[...], w_ref[...], preferred_element_type=F32)
    o_ref[...] = x_ref[...] + _gate(mod_ref, 0) * acc


def _odd_outproj(a, w, x, mod, rows_per_batch, tm):
    r = x.shape[0]
    tpb = rows_per_batch // tm
    row = lambda i: (i, 0)
    return pl.pallas_call(
        _odd_outproj_kernel, grid=(r // tm,),
        in_specs=[pl.BlockSpec((tm, a.shape[1]), row), pl.BlockSpec(w.shape, lambda i: (0, 0)),
                  pl.BlockSpec((tm, D_MODEL), row), pl.BlockSpec((1, 1, 6 * D_MODEL), lambda i: (i // tpb, 0, 0))],
        out_specs=pl.BlockSpec((tm, D_MODEL), row),
        out_shape=jax.ShapeDtypeStruct(x.shape, F32),
        compiler_params=_params(("arbitrary",)), name="odd_outproj",
    )(a, w, x, mod)


def _ffn_kernel(x_ref, mod_ref, n2_ref, wi_ref, wo_ref, o_ref):
    hb = _modulated(x_ref, mod_ref, n2_ref, 1).astype(BF16)
    n_chunks = D_FF // FF_CHUNK

    def gate_up(c):
        return (jnp.dot(hb, wi_ref[:, c * FF_CHUNK:(c + 1) * FF_CHUNK], preferred_element_type=F32),
                jnp.dot(hb, wi_ref[:, D_FF + c * FF_CHUNK:D_FF + (c + 1) * FF_CHUNK], preferred_element_type=F32))

    acc = None
    ahead = gate_up(0)
    for c in range(n_chunks):
        gate, up = ahead
        if c + 1 < n_chunks:
            ahead = gate_up(c + 1)
        act = (gate * _sigmoid(gate) * up).astype(BF16)
        part = jnp.dot(act, wo_ref[c * FF_CHUNK:(c + 1) * FF_CHUNK, :], preferred_element_type=F32)
        acc = part if acc is None else acc + part
    o_ref[...] = x_ref[...] + _gate(mod_ref, 1) * acc


def _ffn(x, mod, norm2, w_in, w_out, rows_per_batch, tm):
    r = x.shape[0]
    tpb = rows_per_batch // tm
    row = lambda i: (i, 0)
    const = lambda i: (0, 0)
    return pl.pallas_call(
        _ffn_kernel, grid=(r // tm,),
        in_specs=[pl.BlockSpec((tm, D_MODEL), row), pl.BlockSpec((1, 1, 6 * D_MODEL), lambda i: (i // tpb, 0, 0)),
                  pl.BlockSpec((1, D_MODEL), const), pl.BlockSpec(w_in.shape, const), pl.BlockSpec(w_out.shape, const)],
        out_specs=pl.BlockSpec((tm, D_MODEL), row),
        out_shape=jax.ShapeDtypeStruct(x.shape, F32),
        compiler_params=_params(("arbitrary",)), name="ffn",
    )(x, mod, norm2, w_in, w_out)


def _rope_tables(n_tokens):
    t = jnp.arange(n_tokens)
    pos = jnp.stack([(t // GRID_W).astype(F32), (t % GRID_W).astype(F32)], axis=1)
    freqs = ROPE_THETA ** (-jnp.arange(ROPE_PAIRS, dtype=F32) / ROPE_PAIRS)
    ang = pos[:, :, None] * freqs
    cos, sin = jnp.cos(ang), jnp.sin(ang)
    zero = jnp.zeros_like(sin)
    lay = lambda first, second: jnp.tile(jnp.stack([first, second], axis=2).reshape(n_tokens, HEAD_DIM), (1, 2))
    return lay(cos, cos), lay(-sin, zero), lay(zero, sin)


def _with_ones(v):
    rows, width = v.shape
    v3 = v.reshape(rows, width // LANES, LANES).astype(BF16)
    return jnp.concatenate([v3, jnp.ones_like(v3)], axis=-1).reshape(rows, 2 * width)


def _trunk(x, mods, li, P, batch, rope_tabs, ctx, tm, prev=None, n_layers=1):
    lj = li // 2
    rows_per_batch = x.shape[0] // batch
    mod_rpb = rows_per_batch if mods.shape[0] > 1 else x.shape[0]
    is_ctx = ctx is None
    tm_wide = min(2 * tm, x.shape[0])
    new = None
    if is_ctx and prev is None:
        if li % 2 == 0:
            state = (batch, n_layers, 2, ML_HEADS)
            shapes = [(batch, n_layers, rows_per_batch, SEG)] * 2 + [
                state + (ML_DIM, ML_DIM), state + (1, ML_DIM), state + (1, ML_DIM)]
        else:
            shapes = [(batch, n_layers, rows_per_batch, GQA_KV_HEADS * HEAD_DIM)] * 2
        prev = tuple(jnp.zeros(shp, F32) for shp in shapes)
    if li % 2 == 0:
        lam_init = 0.8 - 0.6 * math.exp(-0.3 * li)
        outs = _even_inproj(x, mods, P['norm1'][li], P['w_even_main'][lj], P['b_gate'][lj],
                            P['gq_even'][lj], P['gk_even'][lj], rope_tabs, mod_rpb,
                            (rows_per_batch, lj, n_layers, prev[:2] if prev else None) if is_ctx else None, tm)
        q, k, v, mq, mk, mv, mo, gt = outs[:8]
        ck, cv, init = (None, None, None) if is_ctx else ctx
        if is_ctx:
            da = _da_ctx_attention(q, k, v, P['lam_even'][lj], P['da_norm'][lj], batch, lam_init)
        else:
            da = _pipe_attention("da", q, k, v, ck, cv, batch, ATTN_TQ,
                                 extra=(P['lam_even'][lj], P['da_norm'][lj]), lam_init=lam_init)
        hf, hb, c_n, n_n, m_n = _mlstm(mq, mk, mv, gt, init, batch, lj if is_ctx else 0, n_layers,
                                       prev[2:] if prev else None)
        x = _even_outproj(da, hf, hb, mo, P['ml_norm'][lj], P['w_out_even'][lj], x, mods, mod_rpb, tm_wide)
        if is_ctx:
            new = (outs[8], outs[9], c_n, n_n, m_n)
    else:
        outs = _odd_inproj(x, mods, P['norm1'][li], P['w_in_odd'][lj], P['gq_odd'][lj], P['gk_odd'][lj],
                           rope_tabs, mod_rpb, (rows_per_batch, lj, n_layers, prev) if is_ctx else None, tm)
        q, k, v = outs[:3]
        ck, cv = (None, None) if is_ctx else ctx
        if is_ctx:
            a = _gqa_ctx_attention(q, k, v, batch)
        else:
            a = _pipe_attention("gqa", q, k, v, ck, cv, batch, ATTN_TQ)
        x = _odd_outproj(a, P['w_out_odd'][lj], x, mods, mod_rpb, tm_wide)
        if is_ctx:
            new = (outs[3], outs[4])
    x = _ffn(x, mods, P['norm2'][li], P['w_ffn_in'][li], P['w_ffn_out'][li], mod_rpb, tm_wide)
    return x, new


def kernel(x_prompt, x_sample, c, cache_da_k, cache_da_v, state_mlstm_C, state_mlstm_n, state_mlstm_m, cache_gqa_k, cache_gqa_v, c_ctx, norm1, norm2, w_mod, b_mod, w_in_even, b_gate_even, qk_gain_even, lam_even, da_norm_even, ml_norm_even, w_out_even, w_in_odd, qk_gain_odd, w_out_odd, w_ffn_in, w_ffn_out):
    batch, seq, _ = x_prompt.shape
    dbatch, dseq, _ = x_sample.shape
    depth = norm1.shape[0]
    n_even, n_odd = w_in_even.shape[0], w_in_odd.shape[0]
    past = cache_da_k.shape[2]
    tm = 512

    P = dict(
        norm1=norm1.reshape(depth, 1, D_MODEL), norm2=norm2.reshape(depth, 1, D_MODEL),
        w_even_main=jnp.pad(w_in_even, ((0, 0), (0, 0), (0, LANES - N_GATES))).astype(BF16),
        b_gate=jnp.pad(b_gate_even.reshape(n_even, 1, N_GATES), ((0, 0), (0, 0), (0, LANES - N_GATES))),
        gq_even=qk_gain_even[:, 0].reshape(n_even, 1, LANES), gk_even=qk_gain_even[:, 1].reshape(n_even, 1, LANES),
        lam_even=lam_even, da_norm=da_norm_even.reshape(n_even, 1, LANES),
        ml_norm=ml_norm_even.reshape(n_even, 1, ML_DIM),
        w_out_even=w_out_even.astype(BF16), w_in_odd=w_in_odd.astype(BF16),
        gq_odd=jnp.tile(qk_gain_odd[:, 0], (1, 2)).reshape(n_odd, 1, LANES),
        gk_odd=jnp.tile(qk_gain_odd[:, 1], (1, 2)).reshape(n_odd, 1, LANES),
        w_out_odd=w_out_odd.astype(BF16), w_ffn_in=w_ffn_in.astype(BF16), w_ffn_out=w_ffn_out.astype(BF16))

    cond8 = jnp.concatenate([c_ctx[None, :], c, jnp.zeros((8 - 1 - dbatch, D_MODEL), F32)], axis=0)
    mod = _modulation(cond8, w_mod, b_mod)
    rope_tabs = _rope_tables(dseq)

    x = x_prompt.reshape(batch * seq, D_MODEL)
    new_even = new_odd = None
    for li in range(depth):
        mods = mod[li, 0:1].reshape(1, 1, 6 * D_MODEL)
        if li % 2 == 0:
            x, new_even = _trunk(x, mods, li, P, batch, None, None, tm, new_even, n_even)
        else:
            x, new_odd = _trunk(x, mods, li, P, batch, None, None, tm, new_odd, n_odd)
    y_prompt = x.reshape(batch, seq, D_MODEL)

    x = x_sample.reshape(dbatch * dseq, D_MODEL)
    for li in range(depth):
        j = li // 2
        if li % 2 == 0:
            ctx = (cache_da_k[:, j].reshape(dbatch * past, SEG).astype(BF16),
                   _with_ones(cache_da_v[:, j].reshape(dbatch * past, SEG)),
                   (state_mlstm_C[:, j], state_mlstm_n[:, j][:, :, :, None, :],
                    jnp.broadcast_to(state_mlstm_m[:, j][:, :, :, None, None], (dbatch, 2, ML_HEADS, 1, ML_DIM))))
        else:
            ctx = (cache_gqa_k[:, j].reshape(dbatch * past, GQA_KV_HEADS * HEAD_DIM).astype(BF16),
                   _with_ones(cache_gqa_v[:, j].reshape(dbatch * past, GQA_KV_HEADS * HEAD_DIM)))
        x, _ = _trunk(x, mod[li, 1:1 + dbatch].reshape(dbatch, 1, 6 * D_MODEL), li, P, dbatch, rope_tabs, ctx, tm)
    y_sample = x.reshape(dbatch, dseq, D_MODEL)

    new_da_k = new_even[0].reshape(batch, n_even, seq, DA_HEADS, 2, HEAD_DIM)
    new_da_v = new_even[1].reshape(batch, n_even, seq, DA_HEADS, 2 * HEAD_DIM)
    new_c = new_even[2]
    new_n = new_even[3][:, :, :, :, 0, :]
    new_m = new_even[4][:, :, :, :, 0, 0]
    new_gqa_k = new_odd[0].reshape(batch, n_odd, seq, GQA_KV_HEADS, HEAD_DIM)
    new_gqa_v = new_odd[1].reshape(batch, n_odd, seq, GQA_KV_HEADS, HEAD_DIM)
    return (y_prompt, y_sample, new_da_k, new_da_v, new_c, new_n, new_m, new_gqa_k, new_gqa_v)
```
